```python
import math
import jax
import jax.numpy as jnp
from jax import lax
import numpy as np

D_MODEL = 1024
BATCH = 32
SEQ = 2048
DEPTH = 4

GRID_W = 64
CTX_LEN = 256

N_MIXERS = 3
N_A_LAYERS = (DEPTH + 2) // N_MIXERS
N_B_LAYERS = (DEPTH + 1) // N_MIXERS
N_C_LAYERS = DEPTH // N_MIXERS

BLOCK = 128
WINDOW = 128
BAND = BLOCK + 2 * WINDOW

A_HEAD_DIM = 64
A_HEADS = D_MODEL // A_HEAD_DIM
A_KV_HEADS = A_HEADS // 4
A_GROUP = A_HEADS // A_KV_HEADS
A_Q_DIM = A_HEADS * A_HEAD_DIM
A_KV_DIM = A_KV_HEADS * A_HEAD_DIM
A_QKV_DIM = A_Q_DIM + 2 * A_KV_DIM

CHUNK = 128
B_WIDTH = D_MODEL
B_GROUPS = 8
B_GROUP_W = B_WIDTH // B_GROUPS

C_HEAD_DIM = 128
C_HEADS = D_MODEL // C_HEAD_DIM
C_KV_HEADS = C_HEADS // 2
C_GROUP = C_HEADS // C_KV_HEADS
C_Q_DIM = C_HEADS * C_HEAD_DIM
C_KV_DIM = C_KV_HEADS * C_HEAD_DIM
C_QKV_DIM = C_Q_DIM + 2 * C_KV_DIM

FFN_HIDDEN = int(math.ceil(8 * D_MODEL / 3 / 256)) * 256

ROPE_THETA = 10000.0
RMS_EPS = 1e-6
LN_EPS = 1e-5
NEG_INF = -1e30

kernel_name = 'hybrid_interleaved_dit_prefix_ctx'


def rmsnorm(x, g):
    xf = x.astype(jnp.float32)
    y = xf * lax.rsqrt(jnp.mean(xf * xf, axis=-1, keepdims=True) + RMS_EPS)
    return (y * g.astype(jnp.float32)).astype(x.dtype)


def layernorm(x, g, b):
    xf = x.astype(jnp.float32)
    mu = jnp.mean(xf, axis=-1, keepdims=True)
    var = jnp.mean(jnp.square(xf - mu), axis=-1, keepdims=True)
    y = (xf - mu) * lax.rsqrt(var + LN_EPS)
    return (y * g.astype(jnp.float32) + b.astype(jnp.float32)).astype(x.dtype)


def modulate(h, shift, scale):
    return h * (1 + scale) + shift


def axial_rope_tables(n_tokens, head_dim):
    rows = n_tokens // GRID_W
    row_pos = jnp.repeat(jnp.arange(rows, dtype=jnp.float32), GRID_W)
    col_pos = jnp.tile(jnp.arange(GRID_W, dtype=jnp.float32), rows)
    n_freq = head_dim // 4
    inv_freq = ROPE_THETA ** (-jnp.arange(n_freq, dtype=jnp.float32) / n_freq)
    angles = jnp.concatenate([row_pos[:, None] * inv_freq, col_pos[:, None] * inv_freq], axis=-1)
    return jnp.cos(angles), jnp.sin(angles)


def apply_rope(x, cos, sin):
    xf = x.astype(jnp.float32)
    x1, x2 = jnp.split(xf, 2, axis=-1)
    c = cos[None, :, None, :]
    s = sin[None, :, None, :]
    return jnp.concatenate([x1 * c - x2 * s, x2 * c + x1 * s], axis=-1).astype(x.dtype)


def joint_softmax_attend(logit_parts, value_parts, sink=None):
    sizes = [p.shape[-1] for p in logit_parts]
    logits = jnp.concatenate([p.astype(jnp.float32) for p in logit_parts], axis=-1)
    if sink is not None:
        sink_col = jnp.broadcast_to(sink.astype(jnp.float32)[None, :, :, None, None], logits.shape[:-1] + (1,))
        logits = jnp.concatenate([logits, sink_col], axis=-1)
    probs = jax.nn.softmax(logits, axis=-1)
    out = None
    offset = 0
    for size, v in zip(sizes, value_parts):
        p = probs[..., offset:offset + size].astype(v.dtype)
        term = jnp.einsum('bhgqk,bkhd->bqhgd', p, v)
        out = term if out is None else out + term
        offset += size
    return out


def window_sink_mixer(h_lat, h_ctx, w_qkv, w_o, sink, cos, sin, ctx_out):
    bsz, n, _ = h_lat.shape
    n_ctx = h_ctx.shape[1]
    nb = n // BLOCK
    scale = A_HEAD_DIM ** -0.5
    q, k, v = jnp.split(h_lat @ w_qkv, [A_Q_DIM, A_Q_DIM + A_KV_DIM], axis=-1)
    q = apply_rope(q.reshape(bsz, n, A_HEADS, A_HEAD_DIM), cos, sin) * scale
    k = apply_rope(k.reshape(bsz, n, A_KV_HEADS, A_HEAD_DIM), cos, sin)
    v = v.reshape(bsz, n, A_KV_HEADS, A_HEAD_DIM)
    k_c, v_c = jnp.split(h_ctx @ w_qkv[:, A_Q_DIM:], 2, axis=-1)
    k_c = k_c.reshape(bsz, n_ctx, A_KV_HEADS, A_HEAD_DIM)
    v_c = v_c.reshape(bsz, n_ctx, A_KV_HEADS, A_HEAD_DIM)
    sink_g = sink.reshape(A_KV_HEADS, A_GROUP)

    qb = jnp.moveaxis(q.reshape(bsz, nb, BLOCK, A_KV_HEADS, A_GROUP, A_HEAD_DIM), 1, 0)
    pad = ((0, 0), (WINDOW, WINDOW), (0, 0), (0, 0))
    kp = jnp.pad(k, pad)
    vp = jnp.pad(v, pad)

    def attend_block(args):
        q_blk, blk = args
        start = blk * BLOCK
        k_band = lax.dynamic_slice_in_dim(kp, start, BAND, axis=1)
        v_band = lax.dynamic_slice_in_dim(vp, start, BAND, axis=1)
        qpos = start + jnp.arange(BLOCK)
        kpos = start - WINDOW + jnp.arange(BAND)
        mask = (jnp.abs(qpos[:, None] - kpos[None, :]) <= WINDOW) & (kpos[None, :] >= 0) & (kpos[None, :] < n)
        s_band = jnp.einsum('bqhgd,bkhd->bhgqk', q_blk, k_band).astype(jnp.float32)
        s_band = jnp.where(mask, s_band, NEG_INF)
        s_ctx = jnp.einsum('bqhgd,bkhd->bhgqk', q_blk, k_c)
        return joint_softmax_attend([s_band, s_ctx], [v_band, v_c], sink_g)

    o = lax.map(attend_block, (qb, jnp.arange(nb)))
    y_lat = jnp.moveaxis(o, 0, 1).reshape(bsz, n, A_Q_DIM) @ w_o
    y_ctx = None
    if ctx_out:
        q_c = (h_ctx @ w_qkv[:, :A_Q_DIM]).reshape(bsz, n_ctx, A_KV_HEADS, A_GROUP, A_HEAD_DIM) * scale
        s_cc = jnp.einsum('bqhgd,bkhd->bhgqk', q_c, k_c)
        y_ctx = joint_softmax_attend([s_cc], [v_c], sink_g).reshape(bsz, n_ctx, A_Q_DIM) @ w_o
    return y_lat, y_ctx


def chunk_gmlp(h, w_in, b_in, ln_g, ln_b, w_s, b_s, w_o):
    bsz, n, _ = h.shape
    z = jax.nn.gelu(h @ w_in + b_in, approximate=False)
    u, v = jnp.split(z, 2, axis=-1)
    v = layernorm(v, ln_g, ln_b).reshape(bsz, n // CHUNK, CHUNK, B_GROUPS, B_GROUP_W)
    mixed = jnp.einsum('gpq,bnqgc->bnpgc', w_s, v) + b_s.T[None, None, :, :, None]
    return (u * mixed.reshape(bsz, n, B_WIDTH)) @ w_o


def chunk_gmlp_mixer(h_lat, h_ctx, w_in, b_in, ln_g, ln_b, w_s, b_s, w_o, ctx_out):
    y_lat = chunk_gmlp(h_lat, w_in, b_in, ln_g, ln_b, w_s, b_s, w_o)
    y_ctx = chunk_gmlp(h_ctx, w_in, b_in, ln_g, ln_b, w_s, b_s, w_o) if ctx_out else None
    return y_lat, y_ctx


def global_qknorm_mixer(h_lat, h_ctx, w_qkv, w_o, q_g, k_g, cos, sin, ctx_out):
    bsz, n, _ = h_lat.shape
    n_ctx = h_ctx.shape[1]
    nb = n // BLOCK
    scale = C_HEAD_DIM ** -0.5
    q, k, v = jnp.split(h_lat @ w_qkv, [C_Q_DIM, C_Q_DIM + C_KV_DIM], axis=-1)
    q = apply_rope(rmsnorm(q.reshape(bsz, n, C_HEADS, C_HEAD_DIM), q_g), cos, sin) * scale
    k = apply_rope(rmsnorm(k.reshape(bsz, n, C_KV_HEADS, C_HEAD_DIM), k_g), cos, sin)
    v = v.reshape(bsz, n, C_KV_HEADS, C_HEAD_DIM)
    k_c, v_c = jnp.split(h_ctx @ w_qkv[:, C_Q_DIM:], 2, axis=-1)
    k_c = rmsnorm(k_c.reshape(bsz, n_ctx, C_KV_HEADS, C_HEAD_DIM), k_g)
    v_c = v_c.reshape(bsz, n_ctx, C_KV_HEADS, C_HEAD_DIM)

    qb = jnp.moveaxis(q.reshape(bsz, nb, BLOCK, C_KV_HEADS, C_GROUP, C_HEAD_DIM), 1, 0)

    def attend_block(q_blk):
        s_lat = jnp.einsum('bqhgd,bkhd->bhgqk', q_blk, k)
        s_ctx = jnp.einsum('bqhgd,bkhd->bhgqk', q_blk, k_c)
        return joint_softmax_attend([s_lat, s_ctx], [v, v_c])

    o = lax.map(attend_block, qb)
    y_lat = jnp.moveaxis(o, 0, 1).reshape(bsz, n, C_Q_DIM) @ w_o
    y_ctx = None
    if ctx_out:
        q_c = rmsnorm((h_ctx @ w_qkv[:, :C_Q_DIM]).reshape(bsz, n_ctx, C_HEADS, C_HEAD_DIM), q_g) * scale
        q_c = q_c.reshape(bsz, n_ctx, C_KV_HEADS, C_GROUP, C_HEAD_DIM)
        s_cc = jnp.einsum('bqhgd,bkhd->bhgqk', q_c, k_c)
        y_ctx = joint_softmax_attend([s_cc], [v_c]).reshape(bsz, n_ctx, C_Q_DIM) @ w_o
    return y_lat, y_ctx


def swiglu(h, w_in, w_out):
    gate, up = jnp.split(h @ w_in, 2, axis=-1)
    return (jax.nn.silu(gate) * up) @ w_out


def setup_inputs(seed: int = 0) -> dict:
    key = jax.random.key(seed)
    ks = jax.random.split(key, 24)

    def nrm(k, shape, scale):
        return jax.random.normal(k, shape, jnp.float32) * scale

    d = D_MODEL
    return {
        'x': nrm(ks[0], (BATCH, SEQ, d), 1.0),
        'c': nrm(ks[1], (BATCH, d), 1.0),
        'ctx': nrm(ks[2], (BATCH, CTX_LEN, d), 1.0),
        'c_ctx': nrm(ks[3], (d,), 1.0),
        'ada_w': nrm(ks[4], (DEPTH, d, 6 * d), 0.5 * d ** -0.5),
        'ada_b': nrm(ks[5], (DEPTH, 6 * d), 0.01),
        'norm_g': 1.0 + nrm(ks[6], (DEPTH, 4, d), 0.01),
        'ffn_w_in': nrm(ks[7], (DEPTH, d, 2 * FFN_HIDDEN), d ** -0.5),
        'ffn_w_out': nrm(ks[8], (DEPTH, FFN_HIDDEN, d), FFN_HIDDEN ** -0.5),
        'a_w_qkv': nrm(ks[9], (N_A_LAYERS, d, A_QKV_DIM), d ** -0.5),
        'a_w_o': nrm(ks[10], (N_A_LAYERS, A_Q_DIM, d), A_Q_DIM ** -0.5),
        'a_sink': nrm(ks[11], (N_A_LAYERS, A_HEADS), 0.5),
        'b_w_in': nrm(ks[12], (N_B_LAYERS, d, 2 * B_WIDTH), d ** -0.5),
        'b_b_in': nrm(ks[13], (N_B_LAYERS, 2 * B_WIDTH), 0.01),
        'b_ln_g': 1.0 + nrm(ks[14], (N_B_LAYERS, B_WIDTH), 0.01),
        'b_ln_b': nrm(ks[15], (N_B_LAYERS, B_WIDTH), 0.01),
        'b_w_s': nrm(ks[16], (N_B_LAYERS, B_GROUPS, CHUNK, CHUNK), CHUNK ** -0.5),
        'b_b_s': 1.0 + nrm(ks[17], (N_B_LAYERS, B_GROUPS, CHUNK), 0.02),
        'b_w_o': nrm(ks[18], (N_B_LAYERS, B_WIDTH, d), B_WIDTH ** -0.5),
        'c_w_qkv': nrm(ks[19], (N_C_LAYERS, d, C_QKV_DIM), d ** -0.5),
        'c_w_o': nrm(ks[20], (N_C_LAYERS, C_Q_DIM, d), C_Q_DIM ** -0.5),
        'c_q_g': 1.0 + nrm(ks[21], (N_C_LAYERS, C_HEAD_DIM), 0.01),
        'c_k_g': 1.0 + nrm(ks[22], (N_C_LAYERS, C_HEAD_DIM), 0.01),
    }


def reference(x, c, ctx, c_ctx, ada_w, ada_b, norm_g, ffn_w_in, ffn_w_out,
              a_w_qkv, a_w_o, a_sink,
              b_w_in, b_b_in, b_ln_g, b_ln_b, b_w_s, b_b_s, b_w_o,
              c_w_qkv, c_w_o, c_q_g, c_k_g):
    n = x.shape[1]
    cos_a, sin_a = axial_rope_tables(n, A_HEAD_DIM)
    cos_c, sin_c = axial_rope_tables(n, C_HEAD_DIM)
    silu_c = jax.nn.silu(c)
    silu_cc = jax.nn.silu(c_ctx)
    ctx_s = ctx
    for i in range(DEPTH):
        ctx_out = i < DEPTH - 1
        kind = i % N_MIXERS
        j = i // N_MIXERS
        mod_l = silu_c @ ada_w[i] + ada_b[i]
        mod_c = silu_cc @ ada_w[i] + ada_b[i]
        sh_ml, sc_ml, g_ml, sh_fl, sc_fl, g_fl = [m[:, None, :] for m in jnp.split(mod_l, 6, axis=-1)]
        sh_mc, sc_mc, g_mc, sh_fc, sc_fc, g_fc = jnp.split(mod_c, 6, axis=-1)

        h_l = modulate(rmsnorm(x, norm_g[i, 0]), sh_ml, sc_ml)
        h_c = modulate(rmsnorm(ctx_s, norm_g[i, 0]), sh_mc, sc_mc) if (ctx_out or kind != 1) else None
        if kind == 0:
            y_l, y_c = window_sink_mixer(h_l, h_c, a_w_qkv[j], a_w_o[j], a_sink[j], cos_a, sin_a, ctx_out)
        elif kind == 1:
            y_l, y_c = chunk_gmlp_mixer(h_l, h_c, b_w_in[j], b_b_in[j], b_ln_g[j], b_ln_b[j],
                                        b_w_s[j], b_b_s[j], b_w_o[j], ctx_out)
        else:
            y_l, y_c = global_qknorm_mixer(h_l, h_c, c_w_qkv[j], c_w_o[j], c_q_g[j], c_k_g[j],
                                           cos_c, sin_c, ctx_out)

        x = x + g_ml * rmsnorm(y_l, norm_g[i, 1])
        f_l = swiglu(modulate(rmsnorm(x, norm_g[i, 2]), sh_fl, sc_fl), ffn_w_in[i], ffn_w_out[i])
        x = x + g_fl * rmsnorm(f_l, norm_g[i, 3])

        if ctx_out:
            ctx_s = ctx_s + g_mc * rmsnorm(y_c, norm_g[i, 1])
            f_c = swiglu(modulate(rmsnorm(ctx_s, norm_g[i, 2]), sh_fc, sc_fc), ffn_w_in[i], ffn_w_out[i])
            ctx_s = ctx_s + g_fc * rmsnorm(f_c, norm_g[i, 3])
    return x
```

```python
import functools
import math

import jax
import jax.numpy as jnp
from jax import lax
from jax.experimental import pallas as pl
from jax.experimental.pallas import tpu as pltpu

F32 = jnp.float32
BF16 = jnp.bfloat16

LANES = 128
MIB = 1024 * 1024

GRID_W = 64
N_MIXERS = 3
BLOCK = 128
A_HEAD_DIM = 64
C_HEAD_DIM = 128
KV_HEADS = 4
B_GROUPS = 8
ROPE_THETA = 10000.0
RMS_EPS = 1e-6
LN_EPS = 1e-5
NEG_INF = -1e30
MOD_ROWS = 40


def _rms(x, g):
    return x * lax.rsqrt(jnp.mean(x * x, axis=-1, keepdims=True) + RMS_EPS) * g


def _silu(x):
    return x / (1.0 + jnp.exp(-x))


def _const_spec(shape):
    nd = len(shape)
    return pl.BlockSpec(shape, lambda *_: (0,) * nd, pipeline_mode=pl.Buffered(1))


def _params(vmem_mib, n_axes=1):
    return pltpu.CompilerParams(
        dimension_semantics=("parallel",) * n_axes,
        vmem_limit_bytes=vmem_mib * MIB,
    )


def _ada_kernel(c_ref, w_ref, b_ref, o_ref):
    a = _silu(c_ref[...]).astype(BF16)
    o_ref[...] = jnp.dot(a, w_ref[...].astype(BF16), preferred_element_type=F32) + b_ref[...]


def _ada_call(c_rows, ada_w, ada_b):
    depth, d, n6 = ada_w.shape
    tn = 1536
    return pl.pallas_call(
        _ada_kernel,
        grid=(depth, n6 // tn),
        in_specs=[
            pl.BlockSpec((MOD_ROWS, d), lambda i, j: (0, 0)),
            pl.BlockSpec((None, d, tn), lambda i, j: (i, 0, j)),
            pl.BlockSpec((None, 1, tn), lambda i, j: (i, 0, j)),
        ],
        out_specs=pl.BlockSpec((None, MOD_ROWS, tn), lambda i, j: (i, 0, j)),
        out_shape=jax.ShapeDtypeStruct((depth, MOD_ROWS, n6), F32),
        compiler_params=_params(40, 2),
        name="ada_mod",
    )(c_rows, ada_w, ada_b.reshape(depth, 1, n6))


def _qkv_kernel(*refs, kind, rope, want_q, d):
    it = iter(refs)
    x_ref, mod_ref, g_ref, w_ref = next(it), next(it), next(it), next(it)
    cos_ref = sin_ref = qg_ref = kg_ref = q_ref = None
    if rope:
        cos_ref, sin_ref = next(it), next(it)
    if kind == "c":
        qg_ref, kg_ref = next(it), next(it)
    if want_q:
        q_ref = next(it)
    k_ref, v_ref = next(it), next(it)

    shift = mod_ref[:, 0:d]
    scale = mod_ref[:, d:2 * d]
    h = _rms(x_ref[...], g_ref[...]) * (1.0 + scale) + shift
    y = jnp.dot(h.astype(BF16), w_ref[...], preferred_element_type=F32)

    q_w = d if want_q else 0
    kv_w = KV_HEADS * LANES
    if rope:
        cos = cos_ref[...]
        sin = sin_ref[...]
        lane = lax.broadcasted_iota(jnp.int32, cos.shape, 1)
        first_half = (lane & (A_HEAD_DIM - 1)) < (A_HEAD_DIM // 2)

    def rotate(blk):
        if kind == "a":
            fwd = pltpu.roll(blk, LANES - A_HEAD_DIM // 2, axis=1)
            bwd = pltpu.roll(blk, A_HEAD_DIM // 2, axis=1)
            return jnp.where(first_half, fwd, bwd)
        return pltpu.roll(blk, C_HEAD_DIM // 2, axis=1)

    def head_block(col, gain_ref, out_scale):
        blk = y[:, col:col + LANES]
        if kind == "c":
            blk = _rms(blk, gain_ref[...])
        if rope:
            blk = blk * cos + rotate(blk) * sin
        if out_scale != 1.0:
            blk = blk * out_scale
        return blk.astype(BF16)

    head_dim = A_HEAD_DIM if kind == "a" else C_HEAD_DIM
    q_scale = head_dim ** -0.5
    for j in range(q_w // LANES):
        q_ref[:, j * LANES:(j + 1) * LANES] = head_block(j * LANES, qg_ref, q_scale)
    for j in range(KV_HEADS):
        k_ref[:, j * LANES:(j + 1) * LANES] = head_block(q_w + j * LANES, kg_ref, 1.0)
    v_ref[...] = y[:, q_w + kv_w:q_w + 2 * kv_w].astype(BF16)


def _qkv_call(x2, mod3, mod_row, g_row, w, tables, gains, *, kind, want_q, tm, name):
    t_tokens, d = x2.shape
    n_w = w.shape[1]
    kv_w = KV_HEADS * LANES
    rope = tables is not None
    grid = (t_tokens // tm,)
    in_specs = [
        pl.BlockSpec((tm, d), lambda t: (t, 0)),
        pl.BlockSpec((None, 1, mod3.shape[2]), lambda t: (mod_row(t), 0, 0)),
        _const_spec((1, d)),
        _const_spec((d, n_w)),
    ]
    args = [x2, mod3, g_row, w]
    if rope:
        cos, sin = tables
        n_tab = cos.shape[0] // tm
        in_specs += [pl.BlockSpec((tm, LANES), lambda t: (t % n_tab, 0))] * 2
        args += [cos, sin]
    if kind == "c":
        in_specs += [_const_spec((1, LANES))] * 2
        args += list(gains)
    out_specs = []
    out_shape = []
    if want_q:
        out_specs.append(pl.BlockSpec((tm, d), lambda t: (t, 0)))
        out_shape.append(jax.ShapeDtypeStruct((t_tokens, d), BF16))
    for _ in range(2):
        out_specs.append(pl.BlockSpec((tm, kv_w), lambda t: (t, 0)))
        out_shape.append(jax.ShapeDtypeStruct((t_tokens, kv_w), BF16))
    return pl.pallas_call(
        functools.partial(_qkv_kernel, kind=kind, rope=rope, want_q=want_q, d=d),
        grid=grid,
        in_specs=in_specs,
        out_specs=out_specs,
        out_shape=out_shape,
        compiler_params=_params(40),
        name=name,
    )(*args)


def _att_a_kernel(*refs, band, nb):
    if band:
        (sink_ref, q_ref, kp_ref, kc_ref, kn_ref, vp_ref, vc_ref, vn_ref,
         kx_ref, vx_ref, o_ref) = refs
    else:
        sink_ref, q_ref, kx_ref, vx_ref, o_ref = refs
    group = 4
    rows = group * BLOCK
    lane = lax.broadcasted_iota(jnp.int32, (BLOCK, LANES), 1)
    low = lane < A_HEAD_DIM
    zero = jnp.zeros((BLOCK, LANES), BF16)
    if band:
        i = pl.program_id(1)
        p = lax.broadcasted_iota(jnp.int32, (rows, 3 * BLOCK), 0) & (BLOCK - 1)
        c = lax.broadcasted_iota(jnp.int32, (rows, 3 * BLOCK), 1)
        dist = p - c + 2 * BLOCK
        valid = (dist >= 0) & (dist <= 2 * BLOCK)
        valid &= (c >= BLOCK) | (i > 0)
        valid &= (c < 2 * BLOCK) | (i < nb - 1)

    for h in range(KV_HEADS):
        ks = slice(h * LANES, (h + 1) * LANES)
        if band:
            kb = jnp.concatenate([kp_ref[:, ks], kc_ref[:, ks], kn_ref[:, ks], kx_ref[:, ks]], axis=0)
            vb = jnp.concatenate([vp_ref[:, ks], vc_ref[:, ks], vn_ref[:, ks], vx_ref[:, ks]], axis=0)
        else:
            kb = kx_ref[:, ks]
            vb = vx_ref[:, ks]
        b0 = q_ref[:, (2 * h) * LANES:(2 * h + 1) * LANES]
        b1 = q_ref[:, (2 * h + 1) * LANES:(2 * h + 2) * LANES]
        lhs = jnp.concatenate(
            [jnp.where(low, b0, zero), jnp.where(low, zero, b0),
             jnp.where(low, b1, zero), jnp.where(low, zero, b1)], axis=0)
        s = lax.dot_general(lhs, kb, (((1,), (1,)), ((), ())), preferred_element_type=F32)
        sink = jnp.concatenate(
            [jnp.full((BLOCK, 1), sink_ref[group * h + g], F32) for g in range(group)], axis=0)
        if band:
            s_band = jnp.where(valid, s[:, :3 * BLOCK], NEG_INF)
            s_ctx = s[:, 3 * BLOCK:]
            m = jnp.maximum(jnp.max(s_band, axis=1, keepdims=True),
                            jnp.max(s_ctx, axis=1, keepdims=True))
            m = jnp.maximum(m, sink)
            e_band = jnp.exp(s_band - m)
            e_ctx = jnp.exp(s_ctx - m)
            l = (jnp.sum(e_band, axis=1, keepdims=True) + jnp.sum(e_ctx, axis=1, keepdims=True)
                 + jnp.exp(sink - m))
            e = jnp.concatenate([e_band.astype(BF16), e_ctx.astype(BF16)], axis=1)
        else:
            m = jnp.maximum(jnp.max(s, axis=1, keepdims=True), sink)
            e32 = jnp.exp(s - m)
            l = jnp.sum(e32, axis=1, keepdims=True) + jnp.exp(sink - m)
            e = e32.astype(BF16)
        pv = jnp.dot(e, vb, preferred_element_type=F32) / l
        o0 = jnp.where(low, pv[0:BLOCK], pv[BLOCK:2 * BLOCK])
        o1 = jnp.where(low, pv[2 * BLOCK:3 * BLOCK], pv[3 * BLOCK:4 * BLOCK])
        o_ref[:, (2 * h) * LANES:(2 * h + 1) * LANES] = o0.astype(BF16)
        o_ref[:, (2 * h + 1) * LANES:(2 * h + 2) * LANES] = o1.astype(BF16)


def _att_a_call(sink, q, k, v, kx, vx, *, bsz, band, name):
    t_tokens, d = q.shape
    nb = t_tokens // bsz // BLOCK
    n_ctx = kx.shape[0] // bsz
    kv_w = KV_HEADS * LANES
    qspec = pl.BlockSpec((BLOCK, d), lambda b, i: (b * nb + i, 0))
    xspec = pl.BlockSpec((n_ctx, kv_w), lambda b, i: (b, 0))
    sspec = pl.BlockSpec(memory_space=pltpu.SMEM)
    if band:
        prev = pl.BlockSpec((BLOCK, kv_w), lambda b, i: (b * nb + jnp.maximum(i - 1, 0), 0))
        cur = pl.BlockSpec((BLOCK, kv_w), lambda b, i: (b * nb + i, 0))
        nxt = pl.BlockSpec((BLOCK, kv_w), lambda b, i: (b * nb + jnp.minimum(i + 1, nb - 1), 0))
        in_specs = [sspec, qspec, prev, cur, nxt, prev, cur, nxt, xspec, xspec]
        args = (sink, q, k, k, k, v, v, v, kx, vx)
    else:
        in_specs = [sspec, qspec, xspec, xspec]
        args = (sink, q, kx, vx)
    return pl.pallas_call(
        functools.partial(_att_a_kernel, band=band, nb=nb),
        grid=(bsz, nb),
        in_specs=in_specs,
        out_specs=qspec,
        out_shape=jax.ShapeDtypeStruct((t_tokens, d), BF16),
        compiler_params=_params(32, 2),
        name=name,
    )(*args)


def _att_c_kernel(*refs, lat):
    if lat:
        q_ref, k_ref, v_ref, kx_ref, vx_ref, o_ref = refs
    else:
        q_ref, kx_ref, vx_ref, o_ref = refs
    tq = q_ref.shape[0]
    nt = (((1,), (1,)), ((), ()))
    for h in range(KV_HEADS):
        ks = slice(h * LANES, (h + 1) * LANES)
        lhs = jnp.concatenate(
            [q_ref[:, (2 * h) * LANES:(2 * h + 1) * LANES],
             q_ref[:, (2 * h + 1) * LANES:(2 * h + 2) * LANES]], axis=0)
        s_ctx = lax.dot_general(lhs, kx_ref[:, ks], nt, preferred_element_type=F32)
        m = jnp.max(s_ctx, axis=1, keepdims=True)
        if lat:
            s_lat = lax.dot_general(lhs, k_ref[:, ks], nt, preferred_element_type=F32)
            m = jnp.maximum(m, jnp.max(s_lat, axis=1, keepdims=True))
        e_ctx = jnp.exp(s_ctx - m)
        l = jnp.sum(e_ctx, axis=1, keepdims=True)
        pv = jnp.dot(e_ctx.astype(BF16), vx_ref[:, ks], preferred_element_type=F32)
        if lat:
            e_lat = jnp.exp(s_lat - m)
            l = l + jnp.sum(e_lat, axis=1, keepdims=True)
            pv = pv + jnp.dot(e_lat.astype(BF16), v_ref[:, ks], preferred_element_type=F32)
        pv = pv / l
        o_ref[:, (2 * h) * LANES:(2 * h + 1) * LANES] = pv[:tq].astype(BF16)
        o_ref[:, (2 * h + 1) * LANES:(2 * h + 2) * LANES] = pv[tq:].astype(BF16)


def _att_c_call(q, k, v, kx, vx, *, bsz, lat, tq, name):
    t_tokens, d = q.shape
    nq = t_tokens // bsz // tq
    n_ctx = kx.shape[0] // bsz
    kv_w = KV_HEADS * LANES
    qspec = pl.BlockSpec((tq, d), lambda b, i: (b * nq + i, 0))
    xspec = pl.BlockSpec((n_ctx, kv_w), lambda b, i: (b, 0))
    if lat:
        n_lat = k.shape[0] // bsz
        kspec = pl.BlockSpec((n_lat, kv_w), lambda b, i: (b, 0))
        in_specs = [qspec, kspec, kspec, xspec, xspec]
        args = (q, k, v, kx, vx)
    else:
        in_specs = [qspec, xspec, xspec]
        args = (q, kx, vx)
    return pl.pallas_call(
        functools.partial(_att_c_kernel, lat=lat),
        grid=(bsz, nq),
        in_specs=in_specs,
        out_specs=qspec,
        out_shape=jax.ShapeDtypeStruct((t_tokens, d), BF16),
        compiler_params=_params(48, 2),
        name=name,
    )(*args)


def _gmlp_kernel(x_ref, mod_ref, g_ref, win_ref, bin_ref, lng_ref, lnb_ref, ws_ref, bs_ref,
                 a_ref, *, d):
    tm = x_ref.shape[0]
    shift = mod_ref[:, 0:d]
    scale = mod_ref[:, d:2 * d]
    h = _rms(x_ref[...], g_ref[...]) * (1.0 + scale) + shift
    z = jnp.dot(h.astype(BF16), win_ref[...], preferred_element_type=F32) + bin_ref[...]
    z = 0.5 * z * (1.0 + lax.erf(z * (2.0 ** -0.5)))
    width = z.shape[1] // 2
    u = z[:, :width]
    v = z[:, width:]
    mu = jnp.mean(v, axis=-1, keepdims=True)
    vc = v - mu
    var = jnp.mean(vc * vc, axis=-1, keepdims=True)
    vn = (vc * lax.rsqrt(var + LN_EPS) * lng_ref[...] + lnb_ref[...]).astype(BF16)
    n_chunk = tm // BLOCK
    for g in range(B_GROUPS):
        gs = slice(g * LANES, (g + 1) * LANES)
        rhs = jnp.concatenate([vn[c * BLOCK:(c + 1) * BLOCK, gs] for c in range(n_chunk)], axis=1)
        mixed = jnp.dot(ws_ref[g], rhs, preferred_element_type=F32)
        bias = bs_ref[g]
        for c in range(n_chunk):
            rs = slice(c * BLOCK, (c + 1) * BLOCK)
            a_ref[rs, gs] = (u[rs, gs] * (mixed[:, c * BLOCK:(c + 1) * BLOCK] + bias)).astype(BF16)


def _gmlp_call(x2, mod3, mod_row, g_row, w_in, b_in, ln_g, ln_b, w_s, b_s_tile, *, tm, name):
    t_tokens, d = x2.shape
    n_in = w_in.shape[1]
    width = n_in // 2
    return pl.pallas_call(
        functools.partial(_gmlp_kernel, d=d),
        grid=(t_tokens // tm,),
        in_specs=[
            pl.BlockSpec((tm, d), lambda t: (t, 0)),
            pl.BlockSpec((None, 1, mod3.shape[2]), lambda t: (mod_row(t), 0, 0)),
            _const_spec((1, d)),
            _const_spec((d, n_in)),
            _const_spec((1, n_in)),
            _const_spec((1, width)),
            _const_spec((1, width)),
            _const_spec(w_s.shape),
            _const_spec(b_s_tile.shape),
        ],
        out_specs=pl.BlockSpec((tm, width), lambda t: (t, 0)),
        out_shape=jax.ShapeDtypeStruct((t_tokens, width), BF16),
        compiler_params=_params(48),
        name=name,
    )(x2, mod3, g_row, w_in, b_in, ln_g, ln_b, w_s, b_s_tile)


def _post_ffn_kernel(a_ref, x_ref, mod_ref, ng_ref, wo_ref, win_ref, wout_ref, o_ref, *,
                     d, n_chunks):
    g_mix = mod_ref[:, 2 * d:3 * d]
    sh_f = mod_ref[:, 3 * d:4 * d]
    sc_f = mod_ref[:, 4 * d:5 * d]
    g_ffn = mod_ref[:, 5 * d:6 * d]
    y = jnp.dot(a_ref[...], wo_ref[...], preferred_element_type=F32)
    x1 = x_ref[...] + g_mix * _rms(y, ng_ref[1:2, :])
    h = (_rms(x1, ng_ref[2:3, :]) * (1.0 + sc_f) + sh_f).astype(BF16)
    hidden = wout_ref.shape[0]
    fc = hidden // n_chunks
    f = None
    for c in range(n_chunks):
        gate = jnp.dot(h, win_ref[:, c * fc:(c + 1) * fc], preferred_element_type=F32)
        up = jnp.dot(h, win_ref[:, hidden + c * fc:hidden + (c + 1) * fc],
                     preferred_element_type=F32)
        act = (_silu(gate) * up).astype(BF16)
        part = jnp.dot(act, wout_ref[c * fc:(c + 1) * fc, :], preferred_element_type=F32)
        f = part if f is None else f + part
    o_ref[...] = x1 + g_ffn * _rms(f, ng_ref[3:4, :])


def _post_ffn_call(a2, x2, mod3, mod_row, ng, w_o, w_in, w_out, *, tm, n_chunks, name):
    t_tokens, d = x2.shape
    return pl.pallas_call(
        functools.partial(_post_ffn_kernel, d=d, n_chunks=n_chunks),
        grid=(t_tokens // tm,),
        in_specs=[
            pl.BlockSpec((tm, a2.shape[1]), lambda t: (t, 0)),
            pl.BlockSpec((tm, d), lambda t: (t, 0)),
            pl.BlockSpec((None, 1, mod3.shape[2]), lambda t: (mod_row(t), 0, 0)),
            _const_spec(ng.shape),
            _const_spec(w_o.shape),
            _const_spec(w_in.shape),
            _const_spec(w_out.shape),
        ],
        out_specs=pl.BlockSpec((tm, d), lambda t: (t, 0)),
        out_shape=jax.ShapeDtypeStruct((t_tokens, d), F32),
        compiler_params=_params(56),
        name=name,
    )(a2, x2, mod3, ng, w_o, w_in, w_out)


def _rope_tables(n_tokens, head_dim):
    rows = n_tokens // GRID_W
    row_pos = jnp.repeat(jnp.arange(rows, dtype=F32), GRID_W)
    col_pos = jnp.tile(jnp.arange(GRID_W, dtype=F32), rows)
    n_freq = head_dim // 4
    inv_freq = ROPE_THETA ** (-jnp.arange(n_freq, dtype=F32) / n_freq)
    angles = jnp.concatenate([row_pos[:, None] * inv_freq, col_pos[:, None] * inv_freq], axis=-1)
    cos = jnp.cos(angles)
    sin = jnp.sin(angles)
    reps = LANES // head_dim
    cos_l = jnp.tile(jnp.concatenate([cos, cos], axis=-1), (1, reps))
    sin_l = jnp.tile(jnp.concatenate([-sin, sin], axis=-1), (1, reps))
    return cos_l, sin_l


def _dup_heads(w, head_dim):
    d, n = w.shape
    reps = LANES // head_dim
    w = w.reshape(d, n // head_dim, 1, head_dim)
    return jnp.broadcast_to(w, (d, n // head_dim, reps, head_dim)).reshape(d, n * reps)


def kernel(x, c, ctx, c_ctx, ada_w, ada_b, norm_g, ffn_w_in, ffn_w_out, a_w_qkv, a_w_o, a_sink,
           b_w_in, b_b_in, b_ln_g, b_ln_b, b_w_s, b_b_s, b_w_o, c_w_qkv, c_w_o, c_q_g, c_k_g):
    bsz, n, d = x.shape
    n_ctx = ctx.shape[1]
    depth = ada_w.shape[0]
    assert bsz < MOD_ROWS and n % 512 == 0 and n_ctx % 256 == 0 and d % LANES == 0

    tm_lat = 512
    tm_ctx = 256
    lat_tiles = n // tm_lat
    ctx_row = bsz

    c_rows = jnp.zeros((MOD_ROWS, d), F32).at[:bsz].set(c).at[ctx_row].set(c_ctx)
    mod = _ada_call(c_rows, ada_w, ada_b)
    mod3 = mod.reshape(depth * MOD_ROWS, 1, 6 * d)

    tab_a = _rope_tables(n, A_HEAD_DIM)
    tab_c = _rope_tables(n, C_HEAD_DIM)

    xs = x.reshape(bsz * n, d)
    cs = ctx.reshape(bsz * n_ctx, d)
    q_w = d

    for i in range(depth):
        ctx_out = i < depth - 1
        kind = i % N_MIXERS
        j = i // N_MIXERS
        ng = norm_g[i]
        g0 = ng[0:1]

        def lat_row(t, i=i):
            return i * MOD_ROWS + t // lat_tiles

        def ctx_row_fn(t, i=i):
            return i * MOD_ROWS + ctx_row

        if kind == 0:
            w = a_w_qkv[j]
            w_all = jnp.concatenate(
                [w[:, :q_w], _dup_heads(w[:, q_w:], A_HEAD_DIM)], axis=1).astype(BF16)
            q, k, v = _qkv_call(xs, mod3, lat_row, g0, w_all, tab_a, None,
                                kind="a", want_q=True, tm=tm_lat, name=f"qkv_a_lat{i}")
            if ctx_out:
                qx, kx, vx = _qkv_call(cs, mod3, ctx_row_fn, g0, w_all, None, None,
                                       kind="a", want_q=True, tm=tm_ctx, name=f"qkv_a_ctx{i}")
            else:
                kx, vx = _qkv_call(cs, mod3, ctx_row_fn, g0, w_all[:, q_w:], None, None,
                                   kind="a", want_q=False, tm=tm_ctx, name=f"kv_a_ctx{i}")
            sink = a_sink[j]
            a_lat = _att_a_call(sink, q, k, v, kx, vx, bsz=bsz, band=True, name=f"att_a_lat{i}")
            if ctx_out:
                a_ctx = _att_a_call(sink, qx, None, None, kx, vx, bsz=bsz, band=False,
                                    name=f"att_a_ctx{i}")
            w_o = a_w_o[j]
        elif kind == 1:
            w_in = b_w_in[j].astype(BF16)
            b_in = b_b_in[j][None, :]
            ln_g = b_ln_g[j][None, :]
            ln_b = b_ln_b[j][None, :]
            w_s = b_w_s[j].astype(BF16)
            bs_tile = jnp.broadcast_to(b_b_s[j][:, :, None], b_b_s[j].shape + (LANES,))
            a_lat = _gmlp_call(xs, mod3, lat_row, g0, w_in, b_in, ln_g, ln_b, w_s, bs_tile,
                               tm=tm_lat, name=f"gmlp_lat{i}")
            if ctx_out:
                a_ctx = _gmlp_call(cs, mod3, ctx_row_fn, g0, w_in, b_in, ln_g, ln_b, w_s, bs_tile,
                                   tm=tm_ctx, name=f"gmlp_ctx{i}")
            w_o = b_w_o[j]
        else:
            w_all = c_w_qkv[j].astype(BF16)
            gains = (c_q_g[j][None, :], c_k_g[j][None, :])
            q, k, v = _qkv_call(xs, mod3, lat_row, g0, w_all, tab_c, gains,
                                kind="c", want_q=True, tm=tm_lat, name=f"qkv_c_lat{i}")
            if ctx_out:
                qx, kx, vx = _qkv_call(cs, mod3, ctx_row_fn, g0, w_all, None, gains,
                                       kind="c", want_q=True, tm=tm_ctx, name=f"qkv_c_ctx{i}")
            else:
                kx, vx = _qkv_call(cs, mod3, ctx_row_fn, g0, w_all[:, q_w:], None, gains,
                                   kind="c", want_q=False, tm=tm_ctx, name=f"kv_c_ctx{i}")
            a_lat = _att_c_call(q, k, v, kx, vx, bsz=bsz, lat=True, tq=256, name=f"att_c_lat{i}")
            if ctx_out:
                a_ctx = _att_c_call(qx, None, None, kx, vx, bsz=bsz, lat=False, tq=n_ctx,
                                    name=f"att_c_ctx{i}")
            w_o = c_w_o[j]

        w_o = w_o.astype(BF16)
        f_in = ffn_w_in[i].astype(BF16)
        f_out = ffn_w_out[i].astype(BF16)
        xs = _post_ffn_call(a_lat, xs, mod3, lat_row, ng, w_o, f_in, f_out,
                            tm=tm_lat, n_chunks=2, name=f"post_ffn_lat{i}")
        if ctx_out:
            cs = _post_ffn_call(a_ctx, cs, mod3, ctx_row_fn, ng, w_o, f_in, f_out,
                                tm=tm_ctx, n_chunks=2, name=f"post_ffn_ctx{i}")
    return xs.reshape(bsz, n, d)
```

```python
import functools
import math

import jax
import jax.numpy as jnp
from jax import lax
from jax.experimental import pallas as pl
from jax.experimental.pallas import tpu as pltpu

F32 = jnp.float32
BF16 = jnp.bfloat16

LANES = 128
MXU_TILE = 256
MIB = 1024 * 1024

GRID_W = 64
N_MIXERS = 3
BLOCK = 128
A_HEAD_DIM = 64
C_HEAD_DIM = 128
KV_HEADS = 4
B_GROUPS = 8
ROPE_THETA = 10000.0
RMS_EPS = 1e-6
LN_EPS = 1e-5
NEG_INF = -1e30
LOG2E = math.log2(math.e)
MOD_ROWS = 40


def _rms(x, g):
    return x * lax.rsqrt(jnp.mean(x * x, axis=-1, keepdims=True) + RMS_EPS) * g


def _silu(x):
    return x / (1.0 + jnp.exp(-x))


def _const_spec(shape):
    nd = len(shape)
    return pl.BlockSpec(shape, lambda *_: (0,) * nd, pipeline_mode=pl.Buffered(1))


def _params(vmem_mib, n_axes=1):
    return pltpu.CompilerParams(
        dimension_semantics=("parallel",) * n_axes,
        vmem_limit_bytes=vmem_mib * MIB,
    )


def _ada_kernel(c_ref, w_ref, b_ref, o_ref):
    a = _silu(c_ref[...]).astype(BF16)
    o_ref[...] = jnp.dot(a, w_ref[...].astype(BF16), preferred_element_type=F32) + b_ref[...]


def _ada_call(c_rows, ada_w, ada_b):
    depth, d, n6 = ada_w.shape
    tn = 1536
    return pl.pallas_call(
        _ada_kernel,
        grid=(depth, n6 // tn),
        in_specs=[
            pl.BlockSpec((MOD_ROWS, d), lambda i, j: (0, 0)),
            pl.BlockSpec((None, d, tn), lambda i, j: (i, 0, j)),
            pl.BlockSpec((None, 1, tn), lambda i, j: (i, 0, j)),
        ],
        out_specs=pl.BlockSpec((None, MOD_ROWS, tn), lambda i, j: (i, 0, j)),
        out_shape=jax.ShapeDtypeStruct((depth, MOD_ROWS, n6), F32),
        compiler_params=_params(40, 2),
        name="ada_mod",
    )(c_rows, ada_w, ada_b.reshape(depth, 1, n6))


def _qkv_kernel(*refs, kind, rope, want_q, d):
    it = iter(refs)
    x_ref, mod_ref, g_ref, w_ref = next(it), next(it), next(it), next(it)
    cos_ref = sin_ref = qg_ref = kg_ref = q_ref = None
    if rope:
        cos_ref, sin_ref = next(it), next(it)
    if kind == "c":
        qg_ref, kg_ref = next(it), next(it)
    if want_q:
        q_ref = next(it)
    k_ref, v_ref = next(it), next(it)

    shift = mod_ref[:, 0:d]
    scale = mod_ref[:, d:2 * d]
    h = _rms(x_ref[...], g_ref[...]) * (1.0 + scale) + shift
    y = jnp.dot(h.astype(BF16), w_ref[...], preferred_element_type=F32)

    q_w = d if want_q else 0
    kv_w = KV_HEADS * LANES
    if rope:
        cos = cos_ref[...]
        sin = sin_ref[...]
        lane = lax.broadcasted_iota(jnp.int32, cos.shape, 1)
        first_half = (lane & (A_HEAD_DIM - 1)) < (A_HEAD_DIM // 2)

    def rotate(blk):
        if kind == "a":
            fwd = pltpu.roll(blk, LANES - A_HEAD_DIM // 2, axis=1)
            bwd = pltpu.roll(blk, A_HEAD_DIM // 2, axis=1)
            return jnp.where(first_half, fwd, bwd)
        return pltpu.roll(blk, C_HEAD_DIM // 2, axis=1)

    def head_block(col, gain_ref, out_scale):
        blk = y[:, col:col + LANES]
        if kind == "c":
            blk = _rms(blk, gain_ref[...])
        if rope:
            blk = blk * cos + rotate(blk) * sin
        if out_scale != 1.0:
            blk = blk * out_scale
        return blk.astype(BF16)

    head_dim = A_HEAD_DIM if kind == "a" else C_HEAD_DIM
    q_scale = head_dim ** -0.5 * LOG2E
    for j in range(q_w // LANES):
        q_ref[:, j * LANES:(j + 1) * LANES] = head_block(j * LANES, qg_ref, q_scale)
    for j in range(KV_HEADS):
        k_ref[:, j * LANES:(j + 1) * LANES] = head_block(q_w + j * LANES, kg_ref, 1.0)
    vlane = lax.broadcasted_iota(jnp.int32, (y.shape[0], LANES), 1)
    for j in range(KV_HEADS):
        vblk = y[:, q_w + kv_w + j * LANES:q_w + kv_w + (j + 1) * LANES]
        if kind == "a":
            v_ref[:, j * LANES:(j + 1) * LANES] = jnp.where(vlane == A_HEAD_DIM, 1.0, vblk).astype(BF16)
        else:
            v_ref[:, (2 * j) * LANES:(2 * j + 1) * LANES] = vblk.astype(BF16)
            v_ref[:, (2 * j + 1) * LANES:(2 * j + 2) * LANES] = jnp.where(vlane == 0, 1.0, 0.0).astype(BF16)


def _qkv_call(x2, mod3, mod_row, g_row, w, tables, gains, *, kind, want_q, tm, name):
    t_tokens, d = x2.shape
    n_w = w.shape[1]
    kv_w = KV_HEADS * LANES
    rope = tables is not None
    grid = (t_tokens // tm,)
    in_specs = [
        pl.BlockSpec((tm, d), lambda t: (t, 0)),
        pl.BlockSpec((None, 1, mod3.shape[2]), lambda t: (mod_row(t), 0, 0)),
        _const_spec((1, d)),
        _const_spec((d, n_w)),
    ]
    args = [x2, mod3, g_row, w]
    if rope:
        cos, sin = tables
        n_tab = cos.shape[0] // tm
        in_specs += [pl.BlockSpec((tm, LANES), lambda t: (t % n_tab, 0))] * 2
        args += [cos, sin]
    if kind == "c":
        in_specs += [_const_spec((1, LANES))] * 2
        args += list(gains)
    out_specs = []
    out_shape = []
    if want_q:
        out_specs.append(pl.BlockSpec((tm, d), lambda t: (t, 0)))
        out_shape.append(jax.ShapeDtypeStruct((t_tokens, d), BF16))
    for width in (kv_w, kv_w if kind == "a" else 2 * kv_w):
        out_specs.append(pl.BlockSpec((tm, width), lambda t: (t, 0)))
        out_shape.append(jax.ShapeDtypeStruct((t_tokens, width), BF16))
    return pl.pallas_call(
        functools.partial(_qkv_kernel, kind=kind, rope=rope, want_q=want_q, d=d),
        grid=grid,
        in_specs=in_specs,
        out_specs=out_specs,
        out_shape=out_shape,
        compiler_params=_params(40),
        name=name,
    )(*args)


def _att_a_kernel(*refs, band, nb):
    if band:
        (sink_ref, q_ref, kp_ref, kc_ref, kn_ref, vp_ref, vc_ref, vn_ref,
         kx_ref, vx_ref, o_ref) = refs
    else:
        sink_ref, q_ref, kx_ref, vx_ref, o_ref = refs
    group = 4
    rows = group * BLOCK
    chunk = BLOCK // 2
    n_ctx = kx_ref.shape[0]
    lane = lax.broadcasted_iota(jnp.int32, (BLOCK, LANES), 1)
    low = lane < A_HEAD_DIM
    zero = jnp.zeros((BLOCK, LANES), BF16)
    if band:
        i = pl.program_id(1)
        row = lax.broadcasted_iota(jnp.int32, (chunk, BLOCK), 0)
        col = lax.broadcasted_iota(jnp.int32, (chunk, BLOCK), 1)

    for h in range(KV_HEADS):
        ks = slice(h * LANES, (h + 1) * LANES)
        if band:
            kb = jnp.concatenate([kp_ref[:, ks], kc_ref[:, ks], kn_ref[:, ks], kx_ref[:, ks]], axis=0)
            vb = jnp.concatenate([vp_ref[:, ks], vc_ref[:, ks], vn_ref[:, ks], vx_ref[:, ks]], axis=0)
        else:
            kb = kx_ref[:, ks]
            vb = vx_ref[:, ks]
        b0 = q_ref[:, (2 * h) * LANES:(2 * h + 1) * LANES]
        b1 = q_ref[:, (2 * h + 1) * LANES:(2 * h + 2) * LANES]
        lhs = jnp.concatenate(
            [jnp.where(low, b0, zero), jnp.where(low, zero, b0),
             jnp.where(low, b1, zero), jnp.where(low, zero, b1)], axis=0)
        s = lax.dot_general(lhs, kb, (((1,), (1,)), ((), ())), preferred_element_type=F32)
        es = []
        sink_terms = []
        for r0 in range(0, rows, chunk):
            sink = sink_ref[group * h + r0 // BLOCK] * LOG2E
            sc = s[r0:r0 + chunk]
            if band:
                p = row + (r0 % BLOCK)
                parts = [jnp.where((col >= p) & (i > 0), sc[:, 0:BLOCK], NEG_INF),
                         sc[:, BLOCK:2 * BLOCK],
                         jnp.where((col <= p) & (i < nb - 1), sc[:, 2 * BLOCK:3 * BLOCK], NEG_INF)]
                first_ctx = 3 * BLOCK
            else:
                parts = []
                first_ctx = 0
            parts += [sc[:, first_ctx + t * BLOCK:first_ctx + (t + 1) * BLOCK]
                      for t in range(n_ctx // BLOCK)]
            m = functools.reduce(jnp.maximum, parts)
            m = jnp.maximum(jnp.max(m, axis=1, keepdims=True), sink)
            es.append(jnp.concatenate([jnp.exp2(part - m).astype(BF16) for part in parts], axis=1))
            sink_terms.append(jnp.exp2(sink - m))
        pv = jnp.dot(jnp.concatenate(es, axis=0), vb, preferred_element_type=F32)
        l = pv[:, A_HEAD_DIM:A_HEAD_DIM + 1] + jnp.concatenate(sink_terms, axis=0)
        pv = pv / l
        o0 = jnp.where(low, pv[0:BLOCK], pltpu.roll(pv[BLOCK:2 * BLOCK], A_HEAD_DIM, axis=1))
        o1 = jnp.where(low, pv[2 * BLOCK:3 * BLOCK], pltpu.roll(pv[3 * BLOCK:4 * BLOCK], A_HEAD_DIM, axis=1))
        o_ref[:, (2 * h) * LANES:(2 * h + 1) * LANES] = o0.astype(BF16)
        o_ref[:, (2 * h + 1) * LANES:(2 * h + 2) * LANES] = o1.astype(BF16)


def _att_a_call(sink, q, k, v, kx, vx, *, bsz, band, name):
    t_tokens, d = q.shape
    nb = t_tokens // bsz // BLOCK
    n_ctx = kx.shape[0] // bsz
    kv_w = KV_HEADS * LANES
    qspec = pl.BlockSpec((BLOCK, d), lambda b, i: (b * nb + i, 0))
    xspec = pl.BlockSpec((n_ctx, kv_w), lambda b, i: (b, 0))
    sspec = pl.BlockSpec(memory_space=pltpu.SMEM)
    if band:
        prev = pl.BlockSpec((BLOCK, kv_w), lambda b, i: (b * nb + jnp.maximum(i - 1, 0), 0))
        cur = pl.BlockSpec((BLOCK, kv_w), lambda b, i: (b * nb + i, 0))
        nxt = pl.BlockSpec((BLOCK, kv_w), lambda b, i: (b * nb + jnp.minimum(i + 1, nb - 1), 0))
        in_specs = [sspec, qspec, prev, cur, nxt, prev, cur, nxt, xspec, xspec]
        args = (sink, q, k, k, k, v, v, v, kx, vx)
    else:
        in_specs = [sspec, qspec, xspec, xspec]
        args = (sink, q, kx, vx)
    return pl.pallas_call(
        functools.partial(_att_a_kernel, band=band, nb=nb),
        grid=(bsz, nb),
        in_specs=in_specs,
        out_specs=qspec,
        out_shape=jax.ShapeDtypeStruct((t_tokens, d), BF16),
        compiler_params=_params(32, 2),
        name=name,
    )(*args)


def _att_c_kernel(*refs, lat):
    if lat:
        q_ref, k_ref, v_ref, kx_ref, vx_ref, o_ref = refs
    else:
        q_ref, kx_ref, vx_ref, o_ref = refs
    tq = q_ref.shape[0]
    nt = (((1,), (1,)), ((), ()))
    for h in range(KV_HEADS):
        ks = slice(h * LANES, (h + 1) * LANES)
        vs = slice(2 * h * LANES, (2 * h + 2) * LANES)
        lhs = jnp.concatenate(
            [q_ref[:, (2 * h) * LANES:(2 * h + 1) * LANES],
             q_ref[:, (2 * h + 1) * LANES:(2 * h + 2) * LANES]], axis=0)
        s_ctx = lax.dot_general(lhs, kx_ref[:, ks], nt, preferred_element_type=F32)
        m = jnp.max(s_ctx, axis=1, keepdims=True)
        if lat:
            s_lat = lax.dot_general(lhs, k_ref[:, ks], nt, preferred_element_type=F32)
            m = jnp.maximum(m, jnp.max(s_lat, axis=1, keepdims=True))
        pv = jnp.dot(jnp.exp2(s_ctx - m).astype(BF16), vx_ref[:, vs], preferred_element_type=F32)
        if lat:
            pv = pv + jnp.dot(jnp.exp2(s_lat - m).astype(BF16), v_ref[:, vs],
                              preferred_element_type=F32)
        out = pv[:, :LANES] / pv[:, LANES:LANES + 1]
        o_ref[:, (2 * h) * LANES:(2 * h + 1) * LANES] = out[:tq].astype(BF16)
        o_ref[:, (2 * h + 1) * LANES:(2 * h + 2) * LANES] = out[tq:].astype(BF16)


def _att_c_call(q, k, v, kx, vx, *, bsz, lat, tq, name):
    t_tokens, d = q.shape
    nq = t_tokens // bsz // tq
    n_ctx = kx.shape[0] // bsz
    kv_w = KV_HEADS * LANES
    qspec = pl.BlockSpec((tq, d), lambda b, i: (b * nq + i, 0))
    kxspec = pl.BlockSpec((n_ctx, kv_w), lambda b, i: (b, 0))
    vxspec = pl.BlockSpec((n_ctx, 2 * kv_w), lambda b, i: (b, 0))
    if lat:
        n_lat = k.shape[0] // bsz
        kspec = pl.BlockSpec((n_lat, kv_w), lambda b, i: (b, 0))
        vspec = pl.BlockSpec((n_lat, 2 * kv_w), lambda b, i: (b, 0))
        in_specs = [qspec, kspec, vspec, kxspec, vxspec]
        args = (q, k, v, kx, vx)
    else:
        in_specs = [qspec, kxspec, vxspec]
        args = (q, kx, vx)
    return pl.pallas_call(
        functools.partial(_att_c_kernel, lat=lat),
        grid=(bsz, nq),
        in_specs=in_specs,
        out_specs=qspec,
        out_shape=jax.ShapeDtypeStruct((t_tokens, d), BF16),
        compiler_params=_params(48, 2),
        name=name,
    )(*args)


def _gmlp_kernel(x_ref, mod_ref, g_ref, win_ref, bin_ref, lng_ref, lnb_ref, ws_ref, bs_ref,
                 a_ref, *, d):
    tm = x_ref.shape[0]
    shift = mod_ref[:, 0:d]
    scale = mod_ref[:, d:2 * d]
    h = _rms(x_ref[...], g_ref[...]) * (1.0 + scale) + shift
    z = jnp.dot(h.astype(BF16), win_ref[...], preferred_element_type=F32) + bin_ref[...]
    z = 0.5 * z * (1.0 + lax.erf(z * (2.0 ** -0.5)))
    width = z.shape[1] // 2
    u = z[:, :width]
    v = z[:, width:]
    mu = jnp.mean(v, axis=-1, keepdims=True)
    vc = v - mu
    var = jnp.mean(vc * vc, axis=-1, keepdims=True)
    vn = (vc * lax.rsqrt(var + LN_EPS) * lng_ref[...] + lnb_ref[...]).astype(BF16)
    n_chunk = tm // BLOCK
    for g in range(B_GROUPS):
        gs = slice(g * LANES, (g + 1) * LANES)
        rhs = jnp.concatenate([vn[c * BLOCK:(c + 1) * BLOCK, gs] for c in range(n_chunk)], axis=1)
        mixed = jnp.dot(ws_ref[g], rhs, preferred_element_type=F32)
        bias = bs_ref[g]
        for c in range(n_chunk):
            rs = slice(c * BLOCK, (c + 1) * BLOCK)
            a_ref[rs, gs] = (u[rs, gs] * (mixed[:, c * BLOCK:(c + 1) * BLOCK] + bias)).astype(BF16)


def _gmlp_call(x2, mod3, mod_row, g_row, w_in, b_in, ln_g, ln_b, w_s, b_s_tile, *, tm, name):
    t_tokens, d = x2.shape
    n_in = w_in.shape[1]
    width = n_in // 2
    return pl.pallas_call(
        functools.partial(_gmlp_kernel, d=d),
        grid=(t_tokens // tm,),
        in_specs=[
            pl.BlockSpec((tm, d), lambda t: (t, 0)),
            pl.BlockSpec((None, 1, mod3.shape[2]), lambda t: (mod_row(t), 0, 0)),
            _const_spec((1, d)),
            _const_spec((d, n_in)),
            _const_spec((1, n_in)),
            _const_spec((1, width)),
            _const_spec((1, width)),
            _const_spec(w_s.shape),
            _const_spec(b_s_tile.shape),
        ],
        out_specs=pl.BlockSpec((tm, width), lambda t: (t, 0)),
        out_shape=jax.ShapeDtypeStruct((t_tokens, width), BF16),
        compiler_params=_params(48),
        name=name,
    )(x2, mod3, g_row, w_in, b_in, ln_g, ln_b, w_s, b_s_tile)


def _post_ffn_kernel(a_ref, x_ref, mod_ref, ng_ref, wo_ref, win_ref, wout_ref, o_ref, *,
                     d, n_chunks):
    g_mix = mod_ref[:, 2 * d:3 * d]
    sh_f = mod_ref[:, 3 * d:4 * d]
    sc_f = mod_ref[:, 4 * d:5 * d]
    g_ffn = mod_ref[:, 5 * d:6 * d]
    y = jnp.dot(a_ref[...], wo_ref[...], preferred_element_type=F32)
    x1 = x_ref[...] + g_mix * _rms(y, ng_ref[1:2, :])
    h = (_rms(x1, ng_ref[2:3, :]) * (1.0 + sc_f) + sh_f).astype(BF16)
    hidden = wout_ref.shape[0]
    fc = hidden // n_chunks
    f = None
    for c in range(n_chunks):
        gate = jnp.dot(h, win_ref[:, c * fc:(c + 1) * fc], preferred_element_type=F32)
        up = jnp.dot(h, win_ref[:, hidden + c * fc:hidden + (c + 1) * fc],
                     preferred_element_type=F32)
        act = (_silu(gate) * up).astype(BF16)
        part = jnp.dot(act, wout_ref[c * fc:(c + 1) * fc, :], preferred_element_type=F32)
        f = part if f is None else f + part
    o_ref[...] = x1 + g_ffn * _rms(f, ng_ref[3:4, :])


def _post_ffn_call(a2, x2, mod3, mod_row, ng, w_o, w_in, w_out, *, tm, n_chunks, name):
    t_tokens, d = x2.shape
    return pl.pallas_call(
        functools.partial(_post_ffn_kernel, d=d, n_chunks=n_chunks),
        grid=(t_tokens // tm,),
        in_specs=[
            pl.BlockSpec((tm, a2.shape[1]), lambda t: (t, 0)),
            pl.BlockSpec((tm, d), lambda t: (t, 0)),
            pl.BlockSpec((None, 1, mod3.shape[2]), lambda t: (mod_row(t), 0, 0)),
            _const_spec(ng.shape),
            _const_spec(w_o.shape),
            _const_spec(w_in.shape),
            _const_spec(w_out.shape),
        ],
        out_specs=pl.BlockSpec((tm, d), lambda t: (t, 0)),
        out_shape=jax.ShapeDtypeStruct((t_tokens, d), F32),
        compiler_params=_params(56),
        name=name,
    )(a2, x2, mod3, ng, w_o, w_in, w_out)


def _rope_tables(n_tokens, head_dim):
    rows = n_tokens // GRID_W
    row_pos = jnp.repeat(jnp.arange(rows, dtype=F32), GRID_W)
    col_pos = jnp.tile(jnp.arange(GRID_W, dtype=F32), rows)
    n_freq = head_dim // 4
    inv_freq = ROPE_THETA ** (-jnp.arange(n_freq, dtype=F32) / n_freq)
    angles = jnp.concatenate([row_pos[:, None] * inv_freq, col_pos[:, None] * inv_freq], axis=-1)
    cos = jnp.cos(angles)
    sin = jnp.sin(angles)
    reps = LANES // head_dim
    cos_l = jnp.tile(jnp.concatenate([cos, cos], axis=-1), (1, reps))
    sin_l = jnp.tile(jnp.concatenate([-sin, sin], axis=-1), (1, reps))
    return cos_l, sin_l


def _dup_heads(w, head_dim):
    d, n = w.shape
    reps = LANES // head_dim
    w = w.reshape(d, n // head_dim, 1, head_dim)
    return jnp.broadcast_to(w, (d, n // head_dim, reps, head_dim)).reshape(d, n * reps)


def _pad_heads(w, head_dim):
    d, n = w.shape
    w = w.reshape(d, n // head_dim, head_dim)
    return jnp.pad(w, ((0, 0), (0, 0), (0, LANES - head_dim))).reshape(d, n // head_dim * LANES)


def kernel(x, c, ctx, c_ctx, ada_w, ada_b, norm_g, ffn_w_in, ffn_w_out, a_w_qkv, a_w_o, a_sink,
           b_w_in, b_b_in, b_ln_g, b_ln_b, b_w_s, b_b_s, b_w_o, c_w_qkv, c_w_o, c_q_g, c_k_g):
    bsz, n, d = x.shape
    n_ctx = ctx.shape[1]
    depth = ada_w.shape[0]
    assert bsz < MOD_ROWS and n % 512 == 0 and n_ctx % 256 == 0 and d % LANES == 0

    tm_lat = 512
    tm_ctx = 256
    lat_tiles = n // tm_lat
    ctx_row = bsz

    c_rows = jnp.zeros((MOD_ROWS, d), F32).at[:bsz].set(c).at[ctx_row].set(c_ctx)
    mod = _ada_call(c_rows, ada_w, ada_b)
    mod3 = mod.reshape(depth * MOD_ROWS, 1, 6 * d)

    tab_a = _rope_tables(n, A_HEAD_DIM)
    tab_c = _rope_tables(n, C_HEAD_DIM)

    xs = x.reshape(bsz * n, d)
    cs = ctx.reshape(bsz * n_ctx, d)
    q_w = d

    for i in range(depth):
        ctx_out = i < depth - 1
        kind = i % N_MIXERS
        j = i // N_MIXERS
        ng = norm_g[i]
        g0 = ng[0:1]

        def lat_row(t, i=i):
            return i * MOD_ROWS + t // lat_tiles

        def ctx_row_fn(t, i=i):
            return i * MOD_ROWS + ctx_row

        if kind == 0:
            w = a_w_qkv[j]
            kv_cols = KV_HEADS * A_HEAD_DIM
            w_all = jnp.concatenate(
                [w[:, :q_w], _dup_heads(w[:, q_w:q_w + kv_cols], A_HEAD_DIM),
                 _pad_heads(w[:, q_w + kv_cols:], A_HEAD_DIM)], axis=1).astype(BF16)
            q, k, v = _qkv_call(xs, mod3, lat_row, g0, w_all, tab_a, None,
                                kind="a", want_q=True, tm=tm_lat, name=f"qkv_a_lat{i}")
            if ctx_out:
                qx, kx, vx = _qkv_call(cs, mod3, ctx_row_fn, g0, w_all, None, None,
                                       kind="a", want_q=True, tm=tm_ctx, name=f"qkv_a_ctx{i}")
            else:
                kx, vx = _qkv_call(cs, mod3, ctx_row_fn, g0, w_all[:, q_w:], None, None,
                                   kind="a", want_q=False, tm=tm_ctx, name=f"kv_a_ctx{i}")
            sink = a_sink[j]
            a_lat = _att_a_call(sink, q, k, v, kx, vx, bsz=bsz, band=True, name=f"att_a_lat{i}")
            if ctx_out:
                a_ctx = _att_a_call(sink, qx, None, None, kx, vx, bsz=bsz, band=False,
                                    name=f"att_a_ctx{i}")
            w_o = a_w_o[j]
        elif kind == 1:
            w_in = b_w_in[j].astype(BF16)
            b_in = b_b_in[j][None, :]
            ln_g = b_ln_g[j][None, :]
            ln_b = b_ln_b[j][None, :]
            w_s = b_w_s[j].astype(BF16)
            bs_tile = jnp.broadcast_to(b_b_s[j][:, :, None], b_b_s[j].shape + (LANES,))
            a_lat = _gmlp_call(xs, mod3, lat_row, g0, w_in, b_in, ln_g, ln_b, w_s, bs_tile,
                               tm=tm_lat, name=f"gmlp_lat{i}")
            if ctx_out:
                a_ctx = _gmlp_call(cs, mod3, ctx_row_fn, g0, w_in, b_in, ln_g, ln_b, w_s, bs_tile,
                                   tm=tm_ctx, name=f"gmlp_ctx{i}")
            w_o = b_w_o[j]
        else:
            w_all = c_w_qkv[j].astype(BF16)
            gains = (c_q_g[j][None, :], c_k_g[j][None, :])
            q, k, v = _qkv_call(xs, mod3, lat_row, g0, w_all, tab_c, gains,
                                kind="c", want_q=True, tm=tm_lat, name=f"qkv_c_lat{i}")
            if ctx_out:
                qx, kx, vx = _qkv_call(cs, mod3, ctx_row_fn, g0, w_all, None, gains,
                                       kind="c", want_q=True, tm=tm_ctx, name=f"qkv_c_ctx{i}")
            else:
                kx, vx = _qkv_call(cs, mod3, ctx_row_fn, g0, w_all[:, q_w:], None, gains,
                                   kind="c", want_q=False, tm=tm_ctx, name=f"kv_c_ctx{i}")
            a_lat = _att_c_call(q, k, v, kx, vx, bsz=bsz, lat=True, tq=256, name=f"att_c_lat{i}")
            if ctx_out:
                a_ctx = _att_c_call(qx, None, None, kx, vx, bsz=bsz, lat=False, tq=n_ctx,
                                    name=f"att_c_ctx{i}")
            w_o = c_w_o[j]

        w_o = w_o.astype(BF16)
        f_in = ffn_w_in[i].astype(BF16)
        f_out = ffn_w_out[i].astype(BF16)
        n_chunks = f_out.shape[0] // MXU_TILE
        xs = _post_ffn_call(a_lat, xs, mod3, lat_row, ng, w_o, f_in, f_out,
                            tm=tm_lat, n_chunks=n_chunks, name=f"post_ffn_lat{i}")
        if ctx_out:
            cs = _post_ffn_call(a_ctx, cs, mod3, ctx_row_fn, ng, w_o, f_in, f_out,
                                tm=tm_ctx, n_chunks=n_chunks, name=f"post_ffn_ctx{i}")
    return xs.reshape(bsz, n, d)
```

```python
import functools
import math

import jax
import jax.numpy as jnp
from jax import lax
from jax.experimental import pallas as pl
from jax.experimental.pallas import tpu as pltpu

F32 = jnp.float32
BF16 = jnp.bfloat16

LANES = 128
MXU_TILE = 256
MIB = 1024 * 1024

GRID_W = 64
N_MIXERS = 3
BLOCK = 128
A_HEAD_DIM = 64
C_HEAD_DIM = 128
KV_HEADS = 4
B_GROUPS = 8
ROPE_THETA = 10000.0
RMS_EPS = 1e-6
LN_EPS = 1e-5
NEG_INF = -1e30
LOG2E = math.log2(math.e)
C_LOGIT_BOUND = 1.02 * C_HEAD_DIM * C_HEAD_DIM ** -0.5 * LOG2E
F32_SAFE_EXP2_RANGE = 120.0
MOD_ROWS = 40


def _rms(x, g):
    return x * lax.rsqrt(jnp.mean(x * x, axis=-1, keepdims=True) + RMS_EPS) * g


def _silu(x):
    return x / (1.0 + jnp.exp(-x))


def _const_spec(shape):
    nd = len(shape)
    return pl.BlockSpec(shape, lambda *_: (0,) * nd, pipeline_mode=pl.Buffered(1))


def _params(vmem_mib, n_axes=1):
    return pltpu.CompilerParams(
        dimension_semantics=("parallel",) * n_axes,
        vmem_limit_bytes=vmem_mib * MIB,
    )


def _ada_kernel(c_ref, w_ref, b_ref, o_ref):
    a = _silu(c_ref[...]).astype(BF16)
    o_ref[...] = jnp.dot(a, w_ref[...].astype(BF16), preferred_element_type=F32) + b_ref[...]


def _ada_call(c_rows, ada_w, ada_b):
    depth, d, n6 = ada_w.shape
    tn = 1536
    return pl.pallas_call(
        _ada_kernel,
        grid=(depth, n6 // tn),
        in_specs=[
            pl.BlockSpec((MOD_ROWS, d), lambda i, j: (0, 0)),
            pl.BlockSpec((None, d, tn), lambda i, j: (i, 0, j)),
            pl.BlockSpec((None, 1, tn), lambda i, j: (i, 0, j)),
        ],
        out_specs=pl.BlockSpec((None, MOD_ROWS, tn), lambda i, j: (i, 0, j)),
        out_shape=jax.ShapeDtypeStruct((depth, MOD_ROWS, n6), F32),
        compiler_params=_params(40, 2),
        name="ada_mod",
    )(c_rows, ada_w, ada_b.reshape(depth, 1, n6))


def _qkv_kernel(*refs, kind, rope, want_q, d):
    it = iter(refs)
    x_ref, mod_ref, g_ref, w_ref = next(it), next(it), next(it), next(it)
    cos_ref = sin_ref = qg_ref = kg_ref = q_ref = None
    if rope:
        cos_ref, sin_ref = next(it), next(it)
    if kind == "c":
        qg_ref, kg_ref = next(it), next(it)
    if want_q:
        q_ref = next(it)
    k_ref, v_ref = next(it), next(it)

    shift = mod_ref[:, 0:d]
    scale = mod_ref[:, d:2 * d]
    h = _rms(x_ref[...], g_ref[...]) * (1.0 + scale) + shift
    y = jnp.dot(h.astype(BF16), w_ref[...], preferred_element_type=F32)

    q_w = d if want_q else 0
    kv_w = KV_HEADS * LANES
    if rope:
        cos = cos_ref[...]
        sin = sin_ref[...]
        lane = lax.broadcasted_iota(jnp.int32, cos.shape, 1)
        first_half = (lane & (A_HEAD_DIM - 1)) < (A_HEAD_DIM // 2)

    def rotate(blk):
        if kind == "a":
            fwd = pltpu.roll(blk, LANES - A_HEAD_DIM // 2, axis=1)
            bwd = pltpu.roll(blk, A_HEAD_DIM // 2, axis=1)
            return jnp.where(first_half, fwd, bwd)
        return pltpu.roll(blk, C_HEAD_DIM // 2, axis=1)

    def head_block(col, gain_ref, out_scale):
        blk = y[:, col:col + LANES]
        if kind == "c":
            blk = _rms(blk, gain_ref[...])
        if rope:
            blk = blk * cos + rotate(blk) * sin
        if out_scale != 1.0:
            blk = blk * out_scale
        return blk.astype(BF16)

    head_dim = A_HEAD_DIM if kind == "a" else C_HEAD_DIM
    q_scale = head_dim ** -0.5 * LOG2E
    for j in range(q_w // LANES):
        q_ref[:, j * LANES:(j + 1) * LANES] = head_block(j * LANES, qg_ref, q_scale)
    for j in range(KV_HEADS):
        k_ref[:, j * LANES:(j + 1) * LANES] = head_block(q_w + j * LANES, kg_ref, 1.0)
    vlane = lax.broadcasted_iota(jnp.int32, (y.shape[0], LANES), 1)
    for j in range(KV_HEADS):
        vblk = y[:, q_w + kv_w + j * LANES:q_w + kv_w + (j + 1) * LANES]
        if kind == "a":
            v_ref[:, j * LANES:(j + 1) * LANES] = jnp.where(vlane == A_HEAD_DIM, 1.0, vblk).astype(BF16)
        else:
            v_ref[:, (2 * j) * LANES:(2 * j + 1) * LANES] = vblk.astype(BF16)
            v_ref[:, (2 * j + 1) * LANES:(2 * j + 2) * LANES] = jnp.where(vlane == 0, 1.0, 0.0).astype(BF16)


def _qkv_call(x2, mod3, mod_row, g_row, w, tables, gains, *, kind, want_q, tm, name):
    t_tokens, d = x2.shape
    n_w = w.shape[1]
    kv_w = KV_HEADS * LANES
    rope = tables is not None
    grid = (t_tokens // tm,)
    in_specs = [
        pl.BlockSpec((tm, d), lambda t: (t, 0)),
        pl.BlockSpec((None, 1, mod3.shape[2]), lambda t: (mod_row(t), 0, 0)),
        _const_spec((1, d)),
        _const_spec((d, n_w)),
    ]
    args = [x2, mod3, g_row, w]
    if rope:
        cos, sin = tables
        n_tab = cos.shape[0] // tm
        in_specs += [pl.BlockSpec((tm, LANES), lambda t: (t % n_tab, 0))] * 2
        args += [cos, sin]
    if kind == "c":
        in_specs += [_const_spec((1, LANES))] * 2
        args += list(gains)
    out_specs = []
    out_shape = []
    if want_q:
        out_specs.append(pl.BlockSpec((tm, d), lambda t: (t, 0)))
        out_shape.append(jax.ShapeDtypeStruct((t_tokens, d), BF16))
    for width in (kv_w, kv_w if kind == "a" else 2 * kv_w):
        out_specs.append(pl.BlockSpec((tm, width), lambda t: (t, 0)))
        out_shape.append(jax.ShapeDtypeStruct((t_tokens, width), BF16))
    return pl.pallas_call(
        functools.partial(_qkv_kernel, kind=kind, rope=rope, want_q=want_q, d=d),
        grid=grid,
        in_specs=in_specs,
        out_specs=out_specs,
        out_shape=out_shape,
        compiler_params=_params(40),
        name=name,
    )(*args)


def _att_a_kernel(*refs, band, nb):
    if band:
        (sink_ref, q_ref, kp_ref, kc_ref, kn_ref, vp_ref, vc_ref, vn_ref,
         kx_ref, vx_ref, o_ref) = refs
    else:
        sink_ref, q_ref, kx_ref, vx_ref, o_ref = refs
    group = 4
    rows = group * BLOCK
    chunk = BLOCK // 2
    n_ctx = kx_ref.shape[0]
    lane = lax.broadcasted_iota(jnp.int32, (BLOCK, LANES), 1)
    low = lane < A_HEAD_DIM
    zero = jnp.zeros((BLOCK, LANES), BF16)
    if band:
        i = pl.program_id(1)
        row = lax.broadcasted_iota(jnp.int32, (chunk, BLOCK), 0)
        col = lax.broadcasted_iota(jnp.int32, (chunk, BLOCK), 1)

    for h in range(KV_HEADS):
        ks = slice(h * LANES, (h + 1) * LANES)
        if band:
            kb = jnp.concatenate([kp_ref[:, ks], kc_ref[:, ks], kn_ref[:, ks], kx_ref[:, ks]], axis=0)
            vb = jnp.concatenate([vp_ref[:, ks], vc_ref[:, ks], vn_ref[:, ks], vx_ref[:, ks]], axis=0)
        else:
            kb = kx_ref[:, ks]
            vb = vx_ref[:, ks]
        b0 = q_ref[:, (2 * h) * LANES:(2 * h + 1) * LANES]
        b1 = q_ref[:, (2 * h + 1) * LANES:(2 * h + 2) * LANES]
        lhs = jnp.concatenate(
            [jnp.where(low, b0, zero), jnp.where(low, zero, b0),
             jnp.where(low, b1, zero), jnp.where(low, zero, b1)], axis=0)
        s = lax.dot_general(lhs, kb, (((1,), (1,)), ((), ())), preferred_element_type=F32)
        es = []
        sink_terms = []
        for r0 in range(0, rows, chunk):
            sink = sink_ref[group * h + r0 // BLOCK] * LOG2E
            sc = s[r0:r0 + chunk]
            if band:
                p = row + (r0 % BLOCK)
                parts = [jnp.where((col >= p) & (i > 0), sc[:, 0:BLOCK], NEG_INF),
                         sc[:, BLOCK:2 * BLOCK],
                         jnp.where((col <= p) & (i < nb - 1), sc[:, 2 * BLOCK:3 * BLOCK], NEG_INF)]
                first_ctx = 3 * BLOCK
            else:
                parts = []
                first_ctx = 0
            parts += [sc[:, first_ctx + t * BLOCK:first_ctx + (t + 1) * BLOCK]
                      for t in range(n_ctx // BLOCK)]
            m = functools.reduce(jnp.maximum, parts)
            m = jnp.maximum(jnp.max(m, axis=1, keepdims=True), sink)
            es.append(jnp.concatenate([jnp.exp2(part - m).astype(BF16) for part in parts], axis=1))
            sink_terms.append(jnp.exp2(sink - m))
        pv = jnp.dot(jnp.concatenate(es, axis=0), vb, preferred_element_type=F32)
        l = pv[:, A_HEAD_DIM:A_HEAD_DIM + 1] + jnp.concatenate(sink_terms, axis=0)
        pv = pv / l
        o0 = jnp.where(low, pv[0:BLOCK], pltpu.roll(pv[BLOCK:2 * BLOCK], A_HEAD_DIM, axis=1))
        o1 = jnp.where(low, pv[2 * BLOCK:3 * BLOCK], pltpu.roll(pv[3 * BLOCK:4 * BLOCK], A_HEAD_DIM, axis=1))
        o_ref[:, (2 * h) * LANES:(2 * h + 1) * LANES] = o0.astype(BF16)
        o_ref[:, (2 * h + 1) * LANES:(2 * h + 2) * LANES] = o1.astype(BF16)


def _att_a_call(sink, q, k, v, kx, vx, *, bsz, band, name):
    t_tokens, d = q.shape
    nb = t_tokens // bsz // BLOCK
    n_ctx = kx.shape[0] // bsz
    kv_w = KV_HEADS * LANES
    qspec = pl.BlockSpec((BLOCK, d), lambda b, i: (b * nb + i, 0))
    xspec = pl.BlockSpec((n_ctx, kv_w), lambda b, i: (b, 0))
    sspec = pl.BlockSpec(memory_space=pltpu.SMEM)
    if band:
        prev = pl.BlockSpec((BLOCK, kv_w), lambda b, i: (b * nb + jnp.maximum(i - 1, 0), 0))
        cur = pl.BlockSpec((BLOCK, kv_w), lambda b, i: (b * nb + i, 0))
        nxt = pl.BlockSpec((BLOCK, kv_w), lambda b, i: (b * nb + jnp.minimum(i + 1, nb - 1), 0))
        in_specs = [sspec, qspec, prev, cur, nxt, prev, cur, nxt, xspec, xspec]
        args = (sink, q, k, k, k, v, v, v, kx, vx)
    else:
        in_specs = [sspec, qspec, xspec, xspec]
        args = (sink, q, kx, vx)
    return pl.pallas_call(
        functools.partial(_att_a_kernel, band=band, nb=nb),
        grid=(bsz, nb),
        in_specs=in_specs,
        out_specs=qspec,
        out_shape=jax.ShapeDtypeStruct((t_tokens, d), BF16),
        compiler_params=_params(32, 2),
        name=name,
    )(*args)


def _att_c_kernel(*refs, lat, row_max):
    if not row_max:
        shift_ref, refs = refs[0], refs[1:]
    if lat:
        q_ref, k_ref, v_ref, kx_ref, vx_ref, o_ref = refs
    else:
        q_ref, kx_ref, vx_ref, o_ref = refs
    tq = q_ref.shape[0]
    nt = (((1,), (1,)), ((), ()))
    for h in range(KV_HEADS):
        ks = slice(h * LANES, (h + 1) * LANES)
        vs = slice(2 * h * LANES, (2 * h + 2) * LANES)
        lhs = jnp.concatenate(
            [q_ref[:, (2 * h) * LANES:(2 * h + 1) * LANES],
             q_ref[:, (2 * h + 1) * LANES:(2 * h + 2) * LANES]], axis=0)
        s_ctx = lax.dot_general(lhs, kx_ref[:, ks], nt, preferred_element_type=F32)
        if lat:
            s_lat = lax.dot_general(lhs, k_ref[:, ks], nt, preferred_element_type=F32)
        if row_max:
            m = jnp.max(s_ctx, axis=1, keepdims=True)
            if lat:
                m = jnp.maximum(m, jnp.max(s_lat, axis=1, keepdims=True))
        else:
            m = shift_ref[0]
        pv = jnp.dot(jnp.exp2(s_ctx - m).astype(BF16), vx_ref[:, vs], preferred_element_type=F32)
        if lat:
            pv = pv + jnp.dot(jnp.exp2(s_lat - m).astype(BF16), v_ref[:, vs],
                              preferred_element_type=F32)
        out = pv[:, :LANES] / pv[:, LANES:LANES + 1]
        o_ref[:, (2 * h) * LANES:(2 * h + 1) * LANES] = out[:tq].astype(BF16)
        o_ref[:, (2 * h + 1) * LANES:(2 * h + 2) * LANES] = out[tq:].astype(BF16)


def _att_c_call(q, k, v, kx, vx, *, bsz, lat, tq, name, shift=None):
    t_tokens, d = q.shape
    row_max = shift is None
    nq = t_tokens // bsz // tq
    n_ctx = kx.shape[0] // bsz
    kv_w = KV_HEADS * LANES
    qspec = pl.BlockSpec((tq, d), lambda b, i: (b * nq + i, 0))
    kxspec = pl.BlockSpec((n_ctx, kv_w), lambda b, i: (b, 0))
    vxspec = pl.BlockSpec((n_ctx, 2 * kv_w), lambda b, i: (b, 0))
    if lat:
        n_lat = k.shape[0] // bsz
        kspec = pl.BlockSpec((n_lat, kv_w), lambda b, i: (b, 0))
        vspec = pl.BlockSpec((n_lat, 2 * kv_w), lambda b, i: (b, 0))
        in_specs = [qspec, kspec, vspec, kxspec, vxspec]
        args = (q, k, v, kx, vx)
    else:
        in_specs = [qspec, kxspec, vxspec]
        args = (q, kx, vx)
    if not row_max:
        in_specs = [pl.BlockSpec(memory_space=pltpu.SMEM)] + in_specs
        args = (shift.reshape(1).astype(F32),) + args
    return pl.pallas_call(
        functools.partial(_att_c_kernel, lat=lat, row_max=row_max),
        grid=(bsz, nq),
        in_specs=in_specs,
        out_specs=qspec,
        out_shape=jax.ShapeDtypeStruct((t_tokens, d), BF16),
        compiler_params=_params(48, 2),
        name=name,
    )(*args)


def _gmlp_kernel(x_ref, mod_ref, g_ref, win_ref, bin_ref, lng_ref, lnb_ref, ws_ref, bs_ref,
                 a_ref, *, d):
    tm = x_ref.shape[0]
    shift = mod_ref[:, 0:d]
    scale = mod_ref[:, d:2 * d]
    h = _rms(x_ref[...], g_ref[...]) * (1.0 + scale) + shift
    z = jnp.dot(h.astype(BF16), win_ref[...], preferred_element_type=F32) + bin_ref[...]
    z = 0.5 * z * (1.0 + lax.erf(z * (2.0 ** -0.5)))
    width = z.shape[1] // 2
    u = z[:, :width]
    v = z[:, width:]
    mu = jnp.mean(v, axis=-1, keepdims=True)
    vc = v - mu
    var = jnp.mean(vc * vc, axis=-1, keepdims=True)
    vn = (vc * lax.rsqrt(var + LN_EPS) * lng_ref[...] + lnb_ref[...]).astype(BF16)
    n_chunk = tm // BLOCK
    for g in range(B_GROUPS):
        gs = slice(g * LANES, (g + 1) * LANES)
        rhs = jnp.concatenate([vn[c * BLOCK:(c + 1) * BLOCK, gs] for c in range(n_chunk)], axis=1)
        mixed = jnp.dot(ws_ref[g], rhs, preferred_element_type=F32)
        bias = bs_ref[g]
        for c in range(n_chunk):
            rs = slice(c * BLOCK, (c + 1) * BLOCK)
            a_ref[rs, gs] = (u[rs, gs] * (mixed[:, c * BLOCK:(c + 1) * BLOCK] + bias)).astype(BF16)


def _gmlp_call(x2, mod3, mod_row, g_row, w_in, b_in, ln_g, ln_b, w_s, b_s_tile, *, tm, name):
    t_tokens, d = x2.shape
    n_in = w_in.shape[1]
    width = n_in // 2
    return pl.pallas_call(
        functools.partial(_gmlp_kernel, d=d),
        grid=(t_tokens // tm,),
        in_specs=[
            pl.BlockSpec((tm, d), lambda t: (t, 0)),
            pl.BlockSpec((None, 1, mod3.shape[2]), lambda t: (mod_row(t), 0, 0)),
            _const_spec((1, d)),
            _const_spec((d, n_in)),
            _const_spec((1, n_in)),
            _const_spec((1, width)),
            _const_spec((1, width)),
            _const_spec(w_s.shape),
            _const_spec(b_s_tile.shape),
        ],
        out_specs=pl.BlockSpec((tm, width), lambda t: (t, 0)),
        out_shape=jax.ShapeDtypeStruct((t_tokens, width), BF16),
        compiler_params=_params(48),
        name=name,
    )(x2, mod3, g_row, w_in, b_in, ln_g, ln_b, w_s, b_s_tile)


def _post_ffn_kernel(a_ref, x_ref, mod_ref, ng_ref, wo_ref, win_ref, wout_ref, o_ref, *,
                     d, n_chunks):
    g_mix = mod_ref[:, 2 * d:3 * d]
    sh_f = mod_ref[:, 3 * d:4 * d]
    sc_f = mod_ref[:, 4 * d:5 * d]
    g_ffn = mod_ref[:, 5 * d:6 * d]
    y = jnp.dot(a_ref[...], wo_ref[...], preferred_element_type=F32)
    x1 = x_ref[...] + g_mix * _rms(y, ng_ref[1:2, :])
    h = (_rms(x1, ng_ref[2:3, :]) * (1.0 + sc_f) + sh_f).astype(BF16)
    hidden = wout_ref.shape[0]
    fc = hidden // n_chunks
    f = None
    for c in range(n_chunks):
        gate = jnp.dot(h, win_ref[:, c * fc:(c + 1) * fc], preferred_element_type=F32)
        up = jnp.dot(h, win_ref[:, hidden + c * fc:hidden + (c + 1) * fc],
                     preferred_element_type=F32)
        act = (_silu(gate) * up).astype(BF16)
        part = jnp.dot(act, wout_ref[c * fc:(c + 1) * fc, :], preferred_element_type=F32)
        f = part if f is None else f + part
    o_ref[...] = x1 + g_ffn * _rms(f, ng_ref[3:4, :])


def _post_ffn_call(a2, x2, mod3, mod_row, ng, w_o, w_in, w_out, *, tm, n_chunks, name):
    t_tokens, d = x2.shape
    return pl.pallas_call(
        functools.partial(_post_ffn_kernel, d=d, n_chunks=n_chunks),
        grid=(t_tokens // tm,),
        in_specs=[
            pl.BlockSpec((tm, a2.shape[1]), lambda t: (t, 0)),
            pl.BlockSpec((tm, d), lambda t: (t, 0)),
            pl.BlockSpec((None, 1, mod3.shape[2]), lambda t: (mod_row(t), 0, 0)),
            _const_spec(ng.shape),
            _const_spec(w_o.shape),
            _const_spec(w_in.shape),
            _const_spec(w_out.shape),
        ],
        out_specs=pl.BlockSpec((tm, d), lambda t: (t, 0)),
        out_shape=jax.ShapeDtypeStruct((t_tokens, d), F32),
        compiler_params=_params(56),
        name=name,
    )(a2, x2, mod3, ng, w_o, w_in, w_out)


def _rope_tables(n_tokens, head_dim):
    rows = n_tokens // GRID_W
    row_pos = jnp.repeat(jnp.arange(rows, dtype=F32), GRID_W)
    col_pos = jnp.tile(jnp.arange(GRID_W, dtype=F32), rows)
    n_freq = head_dim // 4
    inv_freq = ROPE_THETA ** (-jnp.arange(n_freq, dtype=F32) / n_freq)
    angles = jnp.concatenate([row_pos[:, None] * inv_freq, col_pos[:, None] * inv_freq], axis=-1)
    cos = jnp.cos(angles)
    sin = jnp.sin(angles)
    reps = LANES // head_dim
    cos_l = jnp.tile(jnp.concatenate([cos, cos], axis=-1), (1, reps))
    sin_l = jnp.tile(jnp.concatenate([-sin, sin], axis=-1), (1, reps))
    return cos_l, sin_l


def _dup_heads(w, head_dim):
    d, n = w.shape
    reps = LANES // head_dim
    w = w.reshape(d, n // head_dim, 1, head_dim)
    return jnp.broadcast_to(w, (d, n // head_dim, reps, head_dim)).reshape(d, n * reps)


def _pad_heads(w, head_dim):
    d, n = w.shape
    w = w.reshape(d, n // head_dim, head_dim)
    return jnp.pad(w, ((0, 0), (0, 0), (0, LANES - head_dim))).reshape(d, n // head_dim * LANES)


def kernel(x, c, ctx, c_ctx, ada_w, ada_b, norm_g, ffn_w_in, ffn_w_out, a_w_qkv, a_w_o, a_sink,
           b_w_in, b_b_in, b_ln_g, b_ln_b, b_w_s, b_b_s, b_w_o, c_w_qkv, c_w_o, c_q_g, c_k_g):
    bsz, n, d = x.shape
    n_ctx = ctx.shape[1]
    depth = ada_w.shape[0]
    assert bsz < MOD_ROWS and n % 512 == 0 and n_ctx % 256 == 0 and d % LANES == 0

    tm_lat = 512
    tm_ctx = 512 if (bsz * n_ctx) % 512 == 0 else 256
    lat_tiles = n // tm_lat
    ctx_row = bsz

    c_rows = jnp.zeros((MOD_ROWS, d), F32).at[:bsz].set(c).at[ctx_row].set(c_ctx)
    mod = _ada_call(c_rows, ada_w, ada_b)
    mod3 = mod.reshape(depth * MOD_ROWS, 1, 6 * d)

    tab_a = _rope_tables(n, A_HEAD_DIM)
    tab_c = _rope_tables(n, C_HEAD_DIM)

    xs = x.reshape(bsz * n, d)
    cs = ctx.reshape(bsz * n_ctx, d)
    q_w = d

    for i in range(depth):
        ctx_out = i < depth - 1
        kind = i % N_MIXERS
        j = i // N_MIXERS
        ng = norm_g[i]
        g0 = ng[0:1]

        def lat_row(t, i=i):
            return i * MOD_ROWS + t // lat_tiles

        def ctx_row_fn(t, i=i):
            return i * MOD_ROWS + ctx_row

        if kind == 0:
            w = a_w_qkv[j]
            kv_cols = KV_HEADS * A_HEAD_DIM
            w_all = jnp.concatenate(
                [w[:, :q_w], _dup_heads(w[:, q_w:q_w + kv_cols], A_HEAD_DIM),
                 _pad_heads(w[:, q_w + kv_cols:], A_HEAD_DIM)], axis=1).astype(BF16)
            q, k, v = _qkv_call(xs, mod3, lat_row, g0, w_all, tab_a, None,
                                kind="a", want_q=True, tm=tm_lat, name=f"qkv_a_lat{i}")
            if ctx_out:
                qx, kx, vx = _qkv_call(cs, mod3, ctx_row_fn, g0, w_all, None, None,
                                       kind="a", want_q=True, tm=tm_ctx, name=f"qkv_a_ctx{i}")
            else:
                kx, vx = _qkv_call(cs, mod3, ctx_row_fn, g0, w_all[:, q_w:], None, None,
                                   kind="a", want_q=False, tm=tm_ctx, name=f"kv_a_ctx{i}")
            sink = a_sink[j]
            a_lat = _att_a_call(sink, q, k, v, kx, vx, bsz=bsz, band=True, name=f"att_a_lat{i}")
            if ctx_out:
                a_ctx = _att_a_call(sink, qx, None, None, kx, vx, bsz=bsz, band=False,
                                    name=f"att_a_ctx{i}")
            w_o = a_w_o[j]
        elif kind == 1:
            w_in = b_w_in[j].astype(BF16)
            b_in = b_b_in[j][None, :]
            ln_g = b_ln_g[j][None, :]
            ln_b = b_ln_b[j][None, :]
            w_s = b_w_s[j].astype(BF16)
            bs_tile = jnp.broadcast_to(b_b_s[j][:, :, None], b_b_s[j].shape + (LANES,))
            a_lat = _gmlp_call(xs, mod3, lat_row, g0, w_in, b_in, ln_g, ln_b, w_s, bs_tile,
                               tm=tm_lat, name=f"gmlp_lat{i}")
            if ctx_out:
                a_ctx = _gmlp_call(cs, mod3, ctx_row_fn, g0, w_in, b_in, ln_g, ln_b, w_s, bs_tile,
                                   tm=tm_ctx, name=f"gmlp_ctx{i}")
            w_o = b_w_o[j]
        else:
            w_all = c_w_qkv[j].astype(BF16)
            gains = (c_q_g[j][None, :], c_k_g[j][None, :])
            q, k, v = _qkv_call(xs, mod3, lat_row, g0, w_all, tab_c, gains,
                                kind="c", want_q=True, tm=tm_lat, name=f"qkv_c_lat{i}")
            if ctx_out:
                qx, kx, vx = _qkv_call(cs, mod3, ctx_row_fn, g0, w_all, None, gains,
                                       kind="c", want_q=True, tm=tm_ctx, name=f"qkv_c_ctx{i}")
            else:
                kx, vx = _qkv_call(cs, mod3, ctx_row_fn, g0, w_all[:, q_w:], None, gains,
                                   kind="c", want_q=False, tm=tm_ctx, name=f"kv_c_ctx{i}")
            bound = (C_LOGIT_BOUND * jnp.max(jnp.abs(c_q_g[j])) * jnp.max(jnp.abs(c_k_g[j]))).astype(F32)
            a_lat = lax.cond(
                2.0 * bound < F32_SAFE_EXP2_RANGE,
                lambda: _att_c_call(q, k, v, kx, vx, bsz=bsz, lat=True, tq=256, shift=bound,
                                    name=f"att_c_lat_bounded{i}"),
                lambda: _att_c_call(q, k, v, kx, vx, bsz=bsz, lat=True, tq=256,
                                    name=f"att_c_lat{i}"))
            if ctx_out:
                a_ctx = _att_c_call(qx, None, None, kx, vx, bsz=bsz, lat=False, tq=n_ctx,
                                    name=f"att_c_ctx{i}")
            w_o = c_w_o[j]

        w_o = w_o.astype(BF16)
        f_in = ffn_w_in[i].astype(BF16)
        f_out = ffn_w_out[i].astype(BF16)
        n_chunks = f_out.shape[0] // MXU_TILE
        xs = _post_ffn_call(a_lat, xs, mod3, lat_row, ng, w_o, f_in, f_out,
                            tm=tm_lat, n_chunks=n_chunks, name=f"post_ffn_lat{i}")
        if ctx_out:
            cs = _post_ffn_call(a_ctx, cs, mod3, ctx_row_fn, ng, w_o, f_in, f_out,
                                tm=tm_ctx, n_chunks=n_chunks, name=f"post_ffn_ctx{i}")
    return xs.reshape(bsz, n, d)
```

```python
import functools
import math

import jax
import jax.numpy as jnp
from jax import lax
from jax.experimental import pallas as pl
from jax.experimental.pallas import tpu as pltpu

F32 = jnp.float32
BF16 = jnp.bfloat16

LANES = 128
MXU_TILE = 256
MIB = 1024 * 1024

GRID_W = 64
N_MIXERS = 3
BLOCK = 128
A_HEAD_DIM = 64
C_HEAD_DIM = 128
KV_HEADS = 4
B_GROUPS = 8
ROPE_THETA = 10000.0
RMS_EPS = 1e-6
LN_EPS = 1e-5
NEG_INF = -1e30
LOG2E = math.log2(math.e)
C_LOGIT_BOUND = 1.02 * C_HEAD_DIM * C_HEAD_DIM ** -0.5 * LOG2E
F32_SAFE_EXP2_RANGE = 120.0
MOD_ROWS = 40


def _rms(x, g):
    return x * lax.rsqrt(jnp.mean(x * x, axis=-1, keepdims=True) + RMS_EPS) * g


def _silu(x):
    return x / (1.0 + jnp.exp(-x))


def _const_spec(shape):
    nd = len(shape)
    return pl.BlockSpec(shape, lambda *_: (0,) * nd, pipeline_mode=pl.Buffered(1))


def _params(vmem_mib, n_axes=1):
    return pltpu.CompilerParams(
        dimension_semantics=("parallel",) * n_axes,
        vmem_limit_bytes=vmem_mib * MIB,
    )


def _ada_kernel(c_ref, w_ref, b_ref, o_ref):
    a = _silu(c_ref[...]).astype(BF16)
    o_ref[...] = jnp.dot(a, w_ref[...].astype(BF16), preferred_element_type=F32) + b_ref[...]


def _ada_call(c_rows, ada_w, ada_b):
    depth, d, n6 = ada_w.shape
    tn = 1536
    return pl.pallas_call(
        _ada_kernel,
        grid=(depth, n6 // tn),
        in_specs=[
            pl.BlockSpec((MOD_ROWS, d), lambda i, j: (0, 0)),
            pl.BlockSpec((None, d, tn), lambda i, j: (i, 0, j)),
            pl.BlockSpec((None, 1, tn), lambda i, j: (i, 0, j)),
        ],
        out_specs=pl.BlockSpec((None, MOD_ROWS, tn), lambda i, j: (i, 0, j)),
        out_shape=jax.ShapeDtypeStruct((depth, MOD_ROWS, n6), F32),
        compiler_params=_params(40, 2),
        name="ada_mod",
    )(c_rows, ada_w, ada_b.reshape(depth, 1, n6))


def _qkv_kernel(*refs, kind, rope, want_q, d):
    it = iter(refs)
    x_ref, mod_ref, g_ref, w_ref = next(it), next(it), next(it), next(it)
    cos_ref = sin_ref = qg_ref = kg_ref = q_ref = None
    if rope:
        cos_ref, sin_ref = next(it), next(it)
    if kind == "c":
        qg_ref, kg_ref = next(it), next(it)
    if want_q:
        q_ref = next(it)
    k_ref, v_ref = next(it), next(it)

    shift = mod_ref[:, 0:d]
    scale = mod_ref[:, d:2 * d]
    h = _rms(x_ref[...], g_ref[...]) * (1.0 + scale) + shift
    y = jnp.dot(h.astype(BF16), w_ref[...], preferred_element_type=F32)

    q_w = d if want_q else 0
    kv_w = KV_HEADS * LANES
    if rope:
        cos = cos_ref[...]
        sin = sin_ref[...]
        lane = lax.broadcasted_iota(jnp.int32, cos.shape, 1)
        first_half = (lane & (A_HEAD_DIM - 1)) < (A_HEAD_DIM // 2)

    def rotate(blk):
        if kind == "a":
            fwd = pltpu.roll(blk, LANES - A_HEAD_DIM // 2, axis=1)
            bwd = pltpu.roll(blk, A_HEAD_DIM // 2, axis=1)
            return jnp.where(first_half, fwd, bwd)
        return pltpu.roll(blk, C_HEAD_DIM // 2, axis=1)

    def head_block(col, gain_ref, out_scale):
        blk = y[:, col:col + LANES]
        if kind == "c":
            blk = _rms(blk, gain_ref[...])
        if rope:
            blk = blk * cos + rotate(blk) * sin
        if out_scale != 1.0:
            blk = blk * out_scale
        return blk.astype(BF16)

    head_dim = A_HEAD_DIM if kind == "a" else C_HEAD_DIM
    q_scale = head_dim ** -0.5 * LOG2E
    for j in range(q_w // LANES):
        q_ref[:, j * LANES:(j + 1) * LANES] = head_block(j * LANES, qg_ref, q_scale)
    for j in range(KV_HEADS):
        k_ref[:, j * LANES:(j + 1) * LANES] = head_block(q_w + j * LANES, kg_ref, 1.0)
    vlane = lax.broadcasted_iota(jnp.int32, (y.shape[0], LANES), 1)
    for j in range(KV_HEADS):
        vblk = y[:, q_w + kv_w + j * LANES:q_w + kv_w + (j + 1) * LANES]
        if kind == "a":
            v_ref[:, j * LANES:(j + 1) * LANES] = jnp.where(vlane == A_HEAD_DIM, 1.0, vblk).astype(BF16)
        else:
            v_ref[:, (2 * j) * LANES:(2 * j + 1) * LANES] = vblk.astype(BF16)
            v_ref[:, (2 * j + 1) * LANES:(2 * j + 2) * LANES] = jnp.where(vlane == 0, 1.0, 0.0).astype(BF16)


def _qkv_call(x2, mod3, mod_row, g_row, w, tables, gains, *, kind, want_q, tm, name):
    t_tokens, d = x2.shape
    n_w = w.shape[1]
    kv_w = KV_HEADS * LANES
    rope = tables is not None
    grid = (t_tokens // tm,)
    in_specs = [
        pl.BlockSpec((tm, d), lambda t: (t, 0)),
        pl.BlockSpec((None, 1, mod3.shape[2]), lambda t: (mod_row(t), 0, 0)),
        _const_spec((1, d)),
        _const_spec((d, n_w)),
    ]
    args = [x2, mod3, g_row, w]
    if rope:
        cos, sin = tables
        n_tab = cos.shape[0] // tm
        in_specs += [pl.BlockSpec((tm, LANES), lambda t: (t % n_tab, 0))] * 2
        args += [cos, sin]
    if kind == "c":
        in_specs += [_const_spec((1, LANES))] * 2
        args += list(gains)
    out_specs = []
    out_shape = []
    if want_q:
        out_specs.append(pl.BlockSpec((tm, d), lambda t: (t, 0)))
        out_shape.append(jax.ShapeDtypeStruct((t_tokens, d), BF16))
    for width in (kv_w, kv_w if kind == "a" else 2 * kv_w):
        out_specs.append(pl.BlockSpec((tm, width), lambda t: (t, 0)))
        out_shape.append(jax.ShapeDtypeStruct((t_tokens, width), BF16))
    return pl.pallas_call(
        functools.partial(_qkv_kernel, kind=kind, rope=rope, want_q=want_q, d=d),
        grid=grid,
        in_specs=in_specs,
        out_specs=out_specs,
        out_shape=out_shape,
        compiler_params=_params(40),
        name=name,
    )(*args)


def _att_a_kernel(*refs, band, nb, row_max):
    if not row_max:
        shift_ref, refs = refs[0], refs[1:]
    if band:
        (sink_ref, q_ref, kp_ref, kc_ref, kn_ref, vp_ref, vc_ref, vn_ref,
         kx_ref, vx_ref, o_ref) = refs
    else:
        sink_ref, q_ref, kx_ref, vx_ref, o_ref = refs
    group = 4
    rows = group * BLOCK
    chunk = BLOCK // 2
    n_ctx = kx_ref.shape[0]
    lane = lax.broadcasted_iota(jnp.int32, (BLOCK, LANES), 1)
    low = lane < A_HEAD_DIM
    zero = jnp.zeros((BLOCK, LANES), BF16)
    if band:
        i = pl.program_id(1)
        row = lax.broadcasted_iota(jnp.int32, (chunk, BLOCK), 0)
        col = lax.broadcasted_iota(jnp.int32, (chunk, BLOCK), 1)

    for h in range(KV_HEADS):
        ks = slice(h * LANES, (h + 1) * LANES)
        if band:
            kb = jnp.concatenate([kp_ref[:, ks], kc_ref[:, ks], kn_ref[:, ks], kx_ref[:, ks]], axis=0)
            vb = jnp.concatenate([vp_ref[:, ks], vc_ref[:, ks], vn_ref[:, ks], vx_ref[:, ks]], axis=0)
        else:
            kb = kx_ref[:, ks]
            vb = vx_ref[:, ks]
        b0 = q_ref[:, (2 * h) * LANES:(2 * h + 1) * LANES]
        b1 = q_ref[:, (2 * h + 1) * LANES:(2 * h + 2) * LANES]
        lhs = jnp.concatenate(
            [jnp.where(low, b0, zero), jnp.where(low, zero, b0),
             jnp.where(low, b1, zero), jnp.where(low, zero, b1)], axis=0)
        s = lax.dot_general(lhs, kb, (((1,), (1,)), ((), ())), preferred_element_type=F32)
        es = []
        sink_terms = []
        for r0 in range(0, rows, chunk):
            sink = sink_ref[group * h + r0 // BLOCK] * LOG2E
            sc = s[r0:r0 + chunk]
            if band:
                p = row + (r0 % BLOCK)
                parts = [jnp.where((col >= p) & (i > 0), sc[:, 0:BLOCK], NEG_INF),
                         sc[:, BLOCK:2 * BLOCK],
                         jnp.where((col <= p) & (i < nb - 1), sc[:, 2 * BLOCK:3 * BLOCK], NEG_INF)]
                first_ctx = 3 * BLOCK
            else:
                parts = []
                first_ctx = 0
            parts += [sc[:, first_ctx + t * BLOCK:first_ctx + (t + 1) * BLOCK]
                      for t in range(n_ctx // BLOCK)]
            if row_max:
                m = functools.reduce(jnp.maximum, parts)
                m = jnp.maximum(jnp.max(m, axis=1, keepdims=True), sink)
            else:
                m = jnp.full((chunk, 1), jnp.maximum(shift_ref[0], sink), F32)
            es.append(jnp.concatenate([jnp.exp2(part - m).astype(BF16) for part in parts], axis=1))
            sink_terms.append(jnp.exp2(sink - m))
        pv = jnp.dot(jnp.concatenate(es, axis=0), vb, preferred_element_type=F32)
        l = pv[:, A_HEAD_DIM:A_HEAD_DIM + 1] + jnp.concatenate(sink_terms, axis=0)
        pv = pv / l
        o0 = jnp.where(low, pv[0:BLOCK], pltpu.roll(pv[BLOCK:2 * BLOCK], A_HEAD_DIM, axis=1))
        o1 = jnp.where(low, pv[2 * BLOCK:3 * BLOCK], pltpu.roll(pv[3 * BLOCK:4 * BLOCK], A_HEAD_DIM, axis=1))
        o_ref[:, (2 * h) * LANES:(2 * h + 1) * LANES] = o0.astype(BF16)
        o_ref[:, (2 * h + 1) * LANES:(2 * h + 2) * LANES] = o1.astype(BF16)


def _att_a_call(sink, q, k, v, kx, vx, *, bsz, band, name, shift=None):
    t_tokens, d = q.shape
    row_max = shift is None
    nb = t_tokens // bsz // BLOCK
    n_ctx = kx.shape[0] // bsz
    kv_w = KV_HEADS * LANES
    qspec = pl.BlockSpec((BLOCK, d), lambda b, i: (b * nb + i, 0))
    xspec = pl.BlockSpec((n_ctx, kv_w), lambda b, i: (b, 0))
    sspec = pl.BlockSpec(memory_space=pltpu.SMEM)
    if band:
        prev = pl.BlockSpec((BLOCK, kv_w), lambda b, i: (b * nb + jnp.maximum(i - 1, 0), 0))
        cur = pl.BlockSpec((BLOCK, kv_w), lambda b, i: (b * nb + i, 0))
        nxt = pl.BlockSpec((BLOCK, kv_w), lambda b, i: (b * nb + jnp.minimum(i + 1, nb - 1), 0))
        in_specs = [sspec, qspec, prev, cur, nxt, prev, cur, nxt, xspec, xspec]
        args = (sink, q, k, k, k, v, v, v, kx, vx)
    else:
        in_specs = [sspec, qspec, xspec, xspec]
        args = (sink, q, kx, vx)
    if not row_max:
        in_specs = [sspec] + in_specs
        args = (shift.reshape(1).astype(F32),) + args
    return pl.pallas_call(
        functools.partial(_att_a_kernel, band=band, nb=nb, row_max=row_max),
        grid=(bsz, nb),
        in_specs=in_specs,
        out_specs=qspec,
        out_shape=jax.ShapeDtypeStruct((t_tokens, d), BF16),
        compiler_params=_params(32, 2),
        name=name,
    )(*args)


def _att_c_kernel(*refs, lat, row_max):
    if not row_max:
        shift_ref, refs = refs[0], refs[1:]
    if lat:
        q_ref, k_ref, v_ref, kx_ref, vx_ref, o_ref = refs
    else:
        q_ref, kx_ref, vx_ref, o_ref = refs
    tq = q_ref.shape[0]
    nt = (((1,), (1,)), ((), ()))
    for h in range(KV_HEADS):
        ks = slice(h * LANES, (h + 1) * LANES)
        vs = slice(2 * h * LANES, (2 * h + 2) * LANES)
        lhs = jnp.concatenate(
            [q_ref[:, (2 * h) * LANES:(2 * h + 1) * LANES],
             q_ref[:, (2 * h + 1) * LANES:(2 * h + 2) * LANES]], axis=0)
        s_ctx = lax.dot_general(lhs, kx_ref[:, ks], nt, preferred_element_type=F32)
        if lat:
            s_lat = lax.dot_general(lhs, k_ref[:, ks], nt, preferred_element_type=F32)
        if row_max:
            m = jnp.max(s_ctx, axis=1, keepdims=True)
            if lat:
                m = jnp.maximum(m, jnp.max(s_lat, axis=1, keepdims=True))
        else:
            m = shift_ref[0]
        pv = jnp.dot(jnp.exp2(s_ctx - m).astype(BF16), vx_ref[:, vs], preferred_element_type=F32)
        if lat:
            pv = pv + jnp.dot(jnp.exp2(s_lat - m).astype(BF16), v_ref[:, vs],
                              preferred_element_type=F32)
        out = pv[:, :LANES] / pv[:, LANES:LANES + 1]
        o_ref[:, (2 * h) * LANES:(2 * h + 1) * LANES] = out[:tq].astype(BF16)
        o_ref[:, (2 * h + 1) * LANES:(2 * h + 2) * LANES] = out[tq:].astype(BF16)


def _att_c_call(q, k, v, kx, vx, *, bsz, lat, tq, name, shift=None):
    t_tokens, d = q.shape
    row_max = shift is None
    nq = t_tokens // bsz // tq
    n_ctx = kx.shape[0] // bsz
    kv_w = KV_HEADS * LANES
    qspec = pl.BlockSpec((tq, d), lambda b, i: (b * nq + i, 0))
    kxspec = pl.BlockSpec((n_ctx, kv_w), lambda b, i: (b, 0))
    vxspec = pl.BlockSpec((n_ctx, 2 * kv_w), lambda b, i: (b, 0))
    if lat:
        n_lat = k.shape[0] // bsz
        kspec = pl.BlockSpec((n_lat, kv_w), lambda b, i: (b, 0))
        vspec = pl.BlockSpec((n_lat, 2 * kv_w), lambda b, i: (b, 0))
        in_specs = [qspec, kspec, vspec, kxspec, vxspec]
        args = (q, k, v, kx, vx)
    else:
        in_specs = [qspec, kxspec, vxspec]
        args = (q, kx, vx)
    if not row_max:
        in_specs = [pl.BlockSpec(memory_space=pltpu.SMEM)] + in_specs
        args = (shift.reshape(1).astype(F32),) + args
    return pl.pallas_call(
        functools.partial(_att_c_kernel, lat=lat, row_max=row_max),
        grid=(bsz, nq),
        in_specs=in_specs,
        out_specs=qspec,
        out_shape=jax.ShapeDtypeStruct((t_tokens, d), BF16),
        compiler_params=_params(48, 2),
        name=name,
    )(*args)


def _gmlp_kernel(x_ref, mod_ref, g_ref, win_ref, bin_ref, lng_ref, lnb_ref, ws_ref, bs_ref,
                 a_ref, *, d):
    tm = x_ref.shape[0]
    shift = mod_ref[:, 0:d]
    scale = mod_ref[:, d:2 * d]
    h = _rms(x_ref[...], g_ref[...]) * (1.0 + scale) + shift
    z = jnp.dot(h.astype(BF16), win_ref[...], preferred_element_type=F32) + bin_ref[...]
    z = 0.5 * z * (1.0 + lax.erf(z * (2.0 ** -0.5)))
    width = z.shape[1] // 2
    u = z[:, :width]
    v = z[:, width:]
    mu = jnp.mean(v, axis=-1, keepdims=True)
    vc = v - mu
    var = jnp.mean(vc * vc, axis=-1, keepdims=True)
    vn = (vc * lax.rsqrt(var + LN_EPS) * lng_ref[...] + lnb_ref[...]).astype(BF16)
    n_chunk = tm // BLOCK
    for g in range(B_GROUPS):
        gs = slice(g * LANES, (g + 1) * LANES)
        rhs = jnp.concatenate([vn[c * BLOCK:(c + 1) * BLOCK, gs] for c in range(n_chunk)], axis=1)
        mixed = jnp.dot(ws_ref[g], rhs, preferred_element_type=F32)
        bias = bs_ref[g]
        for c in range(n_chunk):
            rs = slice(c * BLOCK, (c + 1) * BLOCK)
            a_ref[rs, gs] = (u[rs, gs] * (mixed[:, c * BLOCK:(c + 1) * BLOCK] + bias)).astype(BF16)


def _gmlp_call(x2, mod3, mod_row, g_row, w_in, b_in, ln_g, ln_b, w_s, b_s_tile, *, tm, name):
    t_tokens, d = x2.shape
    n_in = w_in.shape[1]
    width = n_in // 2
    return pl.pallas_call(
        functools.partial(_gmlp_kernel, d=d),
        grid=(t_tokens // tm,),
        in_specs=[
            pl.BlockSpec((tm, d), lambda t: (t, 0)),
            pl.BlockSpec((None, 1, mod3.shape[2]), lambda t: (mod_row(t), 0, 0)),
            _const_spec((1, d)),
            _const_spec((d, n_in)),
            _const_spec((1, n_in)),
            _const_spec((1, width)),
            _const_spec((1, width)),
            _const_spec(w_s.shape),
            _const_spec(b_s_tile.shape),
        ],
        out_specs=pl.BlockSpec((tm, width), lambda t: (t, 0)),
        out_shape=jax.ShapeDtypeStruct((t_tokens, width), BF16),
        compiler_params=_params(48),
        name=name,
    )(x2, mod3, g_row, w_in, b_in, ln_g, ln_b, w_s, b_s_tile)


def _post_ffn_kernel(a_ref, x_ref, mod_ref, ng_ref, wo_ref, win_ref, wout_ref, o_ref, *,
                     d, n_chunks):
    g_mix = mod_ref[:, 2 * d:3 * d]
    sh_f = mod_ref[:, 3 * d:4 * d]
    sc_f = mod_ref[:, 4 * d:5 * d]
    g_ffn = mod_ref[:, 5 * d:6 * d]
    y = jnp.dot(a_ref[...], wo_ref[...], preferred_element_type=F32)
    x1 = x_ref[...] + g_mix * _rms(y, ng_ref[1:2, :])
    h = (_rms(x1, ng_ref[2:3, :]) * (1.0 + sc_f) + sh_f).astype(BF16)
    hidden = wout_ref.shape[0]
    fc = hidden // n_chunks
    f = None
    for c in range(n_chunks):
        gate = jnp.dot(h, win_ref[:, c * fc:(c + 1) * fc], preferred_element_type=F32)
        up = jnp.dot(h, win_ref[:, hidden + c * fc:hidden + (c + 1) * fc],
                     preferred_element_type=F32)
        act = (_silu(gate) * up).astype(BF16)
        part = jnp.dot(act, wout_ref[c * fc:(c + 1) * fc, :], preferred_element_type=F32)
        f = part if f is None else f + part
    o_ref[...] = x1 + g_ffn * _rms(f, ng_ref[3:4, :])


def _post_ffn_call(a2, x2, mod3, mod_row, ng, w_o, w_in, w_out, *, tm, n_chunks, name):
    t_tokens, d = x2.shape
    return pl.pallas_call(
        functools.partial(_post_ffn_kernel, d=d, n_chunks=n_chunks),
        grid=(t_tokens // tm,),
        in_specs=[
            pl.BlockSpec((tm, a2.shape[1]), lambda t: (t, 0)),
            pl.BlockSpec((tm, d), lambda t: (t, 0)),
            pl.BlockSpec((None, 1, mod3.shape[2]), lambda t: (mod_row(t), 0, 0)),
            _const_spec(ng.shape),
            _const_spec(w_o.shape),
            _const_spec(w_in.shape),
            _const_spec(w_out.shape),
        ],
        out_specs=pl.BlockSpec((tm, d), lambda t: (t, 0)),
        out_shape=jax.ShapeDtypeStruct((t_tokens, d), F32),
        compiler_params=_params(56),
        name=name,
    )(a2, x2, mod3, ng, w_o, w_in, w_out)


def _rope_tables(n_tokens, head_dim):
    rows = n_tokens // GRID_W
    row_pos = jnp.repeat(jnp.arange(rows, dtype=F32), GRID_W)
    col_pos = jnp.tile(jnp.arange(GRID_W, dtype=F32), rows)
    n_freq = head_dim // 4
    inv_freq = ROPE_THETA ** (-jnp.arange(n_freq, dtype=F32) / n_freq)
    angles = jnp.concatenate([row_pos[:, None] * inv_freq, col_pos[:, None] * inv_freq], axis=-1)
    cos = jnp.cos(angles)
    sin = jnp.sin(angles)
    reps = LANES // head_dim
    cos_l = jnp.tile(jnp.concatenate([cos, cos], axis=-1), (1, reps))
    sin_l = jnp.tile(jnp.concatenate([-sin, sin], axis=-1), (1, reps))
    return cos_l, sin_l


def _max_head_norm2(t, head_lanes):
    tf = t.astype(F32)
    return jnp.max(jnp.sum((tf * tf).reshape(t.shape[0], -1, head_lanes), axis=-1))


def _dup_heads(w, head_dim):
    d, n = w.shape
    reps = LANES // head_dim
    w = w.reshape(d, n // head_dim, 1, head_dim)
    return jnp.broadcast_to(w, (d, n // head_dim, reps, head_dim)).reshape(d, n * reps)


def _pad_heads(w, head_dim):
    d, n = w.shape
    w = w.reshape(d, n // head_dim, head_dim)
    return jnp.pad(w, ((0, 0), (0, 0), (0, LANES - head_dim))).reshape(d, n // head_dim * LANES)


def kernel(x, c, ctx, c_ctx, ada_w, ada_b, norm_g, ffn_w_in, ffn_w_out, a_w_qkv, a_w_o, a_sink,
           b_w_in, b_b_in, b_ln_g, b_ln_b, b_w_s, b_b_s, b_w_o, c_w_qkv, c_w_o, c_q_g, c_k_g):
    bsz, n, d = x.shape
    n_ctx = ctx.shape[1]
    depth = ada_w.shape[0]
    assert bsz < MOD_ROWS and n % 512 == 0 and n_ctx % 256 == 0 and d % LANES == 0

    tm_lat = 512
    tm_ctx = 512 if (bsz * n_ctx) % 512 == 0 else 256
    lat_tiles = n // tm_lat
    ctx_row = bsz

    c_rows = jnp.zeros((MOD_ROWS, d), F32).at[:bsz].set(c).at[ctx_row].set(c_ctx)
    mod = _ada_call(c_rows, ada_w, ada_b)
    mod3 = mod.reshape(depth * MOD_ROWS, 1, 6 * d)

    tab_a = _rope_tables(n, A_HEAD_DIM)
    tab_c = _rope_tables(n, C_HEAD_DIM)

    xs = x.reshape(bsz * n, d)
    cs = ctx.reshape(bsz * n_ctx, d)
    q_w = d

    for i in range(depth):
        ctx_out = i < depth - 1
        kind = i % N_MIXERS
        j = i // N_MIXERS
        ng = norm_g[i]
        g0 = ng[0:1]

        def lat_row(t, i=i):
            return i * MOD_ROWS + t // lat_tiles

        def ctx_row_fn(t, i=i):
            return i * MOD_ROWS + ctx_row

        if kind == 0:
            w = a_w_qkv[j]
            kv_cols = KV_HEADS * A_HEAD_DIM
            w_all = jnp.concatenate(
                [w[:, :q_w], _dup_heads(w[:, q_w:q_w + kv_cols], A_HEAD_DIM),
                 _pad_heads(w[:, q_w + kv_cols:], A_HEAD_DIM)], axis=1).astype(BF16)
            q, k, v = _qkv_call(xs, mod3, lat_row, g0, w_all, tab_a, None,
                                kind="a", want_q=True, tm=tm_lat, name=f"qkv_a_lat{i}")
            if ctx_out:
                qx, kx, vx = _qkv_call(cs, mod3, ctx_row_fn, g0, w_all, None, None,
                                       kind="a", want_q=True, tm=tm_ctx, name=f"qkv_a_ctx{i}")
            else:
                kx, vx = _qkv_call(cs, mod3, ctx_row_fn, g0, w_all[:, q_w:], None, None,
                                   kind="a", want_q=False, tm=tm_ctx, name=f"kv_a_ctx{i}")
            sink = a_sink[j]
            k_norm2 = 0.5 * jnp.maximum(_max_head_norm2(k, LANES), _max_head_norm2(kx, LANES))
            bound = 1.02 * jnp.sqrt(_max_head_norm2(q, A_HEAD_DIM) * k_norm2)
            a_lat = lax.cond(
                2.0 * bound < F32_SAFE_EXP2_RANGE,
                lambda: _att_a_call(sink, q, k, v, kx, vx, bsz=bsz, band=True, shift=bound,
                                    name=f"att_a_lat_bounded{i}"),
                lambda: _att_a_call(sink, q, k, v, kx, vx, bsz=bsz, band=True,
                                    name=f"att_a_lat{i}"))
            if ctx_out:
                a_ctx = _att_a_call(sink, qx, None, None, kx, vx, bsz=bsz, band=False,
                                    name=f"att_a_ctx{i}")
            w_o = a_w_o[j]
        elif kind == 1:
            w_in = b_w_in[j].astype(BF16)
            b_in = b_b_in[j][None, :]
            ln_g = b_ln_g[j][None, :]
            ln_b = b_ln_b[j][None, :]
            w_s = b_w_s[j].astype(BF16)
            bs_tile = jnp.broadcast_to(b_b_s[j][:, :, None], b_b_s[j].shape + (LANES,))
            a_lat = _gmlp_call(xs, mod3, lat_row, g0, w_in, b_in, ln_g, ln_b, w_s, bs_tile,
                               tm=tm_lat, name=f"gmlp_lat{i}")
            if ctx_out:
                a_ctx = _gmlp_call(cs, mod3, ctx_row_fn, g0, w_in, b_in, ln_g, ln_b, w_s, bs_tile,
                                   tm=tm_ctx, name=f"gmlp_ctx{i}")
            w_o = b_w_o[j]
        else:
            w_all = c_w_qkv[j].astype(BF16)
            gains = (c_q_g[j][None, :], c_k_g[j][None, :])
            q, k, v = _qkv_call(xs, mod3, lat_row, g0, w_all, tab_c, gains,
                                kind="c", want_q=True, tm=tm_lat, name=f"qkv_c_lat{i}")
            if ctx_out:
                qx, kx, vx = _qkv_call(cs, mod3, ctx_row_fn, g0, w_all, None, gains,
                                       kind="c", want_q=True, tm=tm_ctx, name=f"qkv_c_ctx{i}")
            else:
                kx, vx = _qkv_call(cs, mod3, ctx_row_fn, g0, w_all[:, q_w:], None, gains,
                                   kind="c", want_q=False, tm=tm_ctx, name=f"kv_c_ctx{i}")
            bound = (C_LOGIT_BOUND * jnp.max(jnp.abs(c_q_g[j])) * jnp.max(jnp.abs(c_k_g[j]))).astype(F32)
            a_lat = lax.cond(
                2.0 * bound < F32_SAFE_EXP2_RANGE,
                lambda: _att_c_call(q, k, v, kx, vx, bsz=bsz, lat=True, tq=256, shift=bound,
                                    name=f"att_c_lat_bounded{i}"),
                lambda: _att_c_call(q, k, v, kx, vx, bsz=bsz, lat=True, tq=256,
                                    name=f"att_c_lat{i}"))
            if ctx_out:
                a_ctx = _att_c_call(qx, None, None, kx, vx, bsz=bsz, lat=False, tq=n_ctx,
                                    name=f"att_c_ctx{i}")
            w_o = c_w_o[j]

        w_o = w_o.astype(BF16)
        f_in = ffn_w_in[i].astype(BF16)
        f_out = ffn_w_out[i].astype(BF16)
        n_chunks = f_out.shape[0] // MXU_TILE
        xs = _post_ffn_call(a_lat, xs, mod3, lat_row, ng, w_o, f_in, f_out,
                            tm=tm_lat, n_chunks=n_chunks, name=f"post_ffn_lat{i}")
        if ctx_out:
            cs = _post_ffn_call(a_ctx, cs, mod3, ctx_row_fn, ng, w_o, f_in, f_out,
                                tm=tm_ctx, n_chunks=n_chunks, name=f"post_ffn_ctx{i}")
    return xs.reshape(bsz, n, d)
```

```python
import functools
import math

import jax
import jax.numpy as jnp
from jax import lax
from jax.experimental import pallas as pl
from jax.experimental.pallas import tpu as pltpu

F32 = jnp.float32
BF16 = jnp.bfloat16

LANES = 128
SUBLANES = 8
MXU_TILE = 256
MIB = 1024 * 1024

GRID_W = 64
N_MIXERS = 3
BLOCK = 128
A_HEAD_DIM = 64
C_HEAD_DIM = 128
KV_HEADS = 4
B_GROUPS = 8
ROPE_THETA = 10000.0
RMS_EPS = 1e-6
LN_EPS = 1e-5
NEG_INF = -1e30
LOG2E = math.log2(math.e)
C_LOGIT_BOUND = 1.02 * C_HEAD_DIM * C_HEAD_DIM ** -0.5 * LOG2E
F32_SAFE_EXP2_RANGE = 120.0
A_SHIFT_HEADROOM = 100.0
MOD_ROWS = 40


def _rms(x, g):
    return x * lax.rsqrt(jnp.mean(x * x, axis=-1, keepdims=True) + RMS_EPS) * g


def _silu(x):
    return x / (1.0 + jnp.exp(-x))


def _const_spec(shape):
    nd = len(shape)
    return pl.BlockSpec(shape, lambda *_: (0,) * nd, pipeline_mode=pl.Buffered(1))


def _params(vmem_mib, n_axes=1):
    return pltpu.CompilerParams(
        dimension_semantics=("parallel",) * n_axes,
        vmem_limit_bytes=vmem_mib * MIB,
    )


def _ada_kernel(c_ref, w_ref, b_ref, o_ref):
    a = _silu(c_ref[...]).astype(BF16)
    o_ref[...] = jnp.dot(a, w_ref[...].astype(BF16), preferred_element_type=F32) + b_ref[...]


def _ada_call(c_rows, ada_w, ada_b):
    depth, d, n6 = ada_w.shape
    tn = 1536
    return pl.pallas_call(
        _ada_kernel,
        grid=(depth, n6 // tn),
        in_specs=[
            pl.BlockSpec((MOD_ROWS, d), lambda i, j: (0, 0)),
            pl.BlockSpec((None, d, tn), lambda i, j: (i, 0, j)),
            pl.BlockSpec((None, 1, tn), lambda i, j: (i, 0, j)),
        ],
        out_specs=pl.BlockSpec((None, MOD_ROWS, tn), lambda i, j: (i, 0, j)),
        out_shape=jax.ShapeDtypeStruct((depth, MOD_ROWS, n6), F32),
        compiler_params=_params(40, 2),
        name="ada_mod",
    )(c_rows, ada_w, ada_b.reshape(depth, 1, n6))


def _qkv_kernel(*refs, kind, rope, want_q, d):
    it = iter(refs)
    x_ref, mod_ref, g_ref, w_ref = next(it), next(it), next(it), next(it)
    cos_ref = sin_ref = qg_ref = kg_ref = q_ref = None
    if rope:
        cos_ref, sin_ref = next(it), next(it)
    if kind == "c":
        qg_ref, kg_ref = next(it), next(it)
    else:
        sel_ref = next(it)
    if want_q:
        q_ref = next(it)
    k_ref, v_ref = next(it), next(it)
    if kind == "a":
        nrm_ref = next(it)

    shift = mod_ref[:, 0:d]
    scale = mod_ref[:, d:2 * d]
    h = _rms(x_ref[...], g_ref[...]) * (1.0 + scale) + shift
    y = jnp.dot(h.astype(BF16), w_ref[...], preferred_element_type=F32)

    q_w = d if want_q else 0
    kv_w = KV_HEADS * LANES
    if rope:
        cos = cos_ref[...]
        sin = sin_ref[...]
        lane = lax.broadcasted_iota(jnp.int32, cos.shape, 1)
        first_half = (lane & (A_HEAD_DIM - 1)) < (A_HEAD_DIM // 2)

    def rotate(blk):
        if kind == "a":
            fwd = pltpu.roll(blk, LANES - A_HEAD_DIM // 2, axis=1)
            bwd = pltpu.roll(blk, A_HEAD_DIM // 2, axis=1)
            return jnp.where(first_half, fwd, bwd)
        return pltpu.roll(blk, C_HEAD_DIM // 2, axis=1)

    def head_block(col, gain_ref, out_scale):
        blk = y[:, col:col + LANES]
        if kind == "c":
            blk = _rms(blk, gain_ref[...])
        if rope:
            blk = blk * cos + rotate(blk) * sin
        if out_scale != 1.0:
            blk = blk * out_scale
        return blk

    def fold_sq(acc, blk):
        sq = blk * blk
        return sq if acc is None else jnp.maximum(acc, sq)

    head_dim = A_HEAD_DIM if kind == "a" else C_HEAD_DIM
    q_scale = head_dim ** -0.5 * LOG2E
    zq = zk = None
    for j in range(q_w // LANES):
        blk = head_block(j * LANES, qg_ref, q_scale)
        q_ref[:, j * LANES:(j + 1) * LANES] = blk.astype(BF16)
        if kind == "a":
            zq = fold_sq(zq, blk)
    for j in range(KV_HEADS):
        blk = head_block(q_w + j * LANES, kg_ref, 1.0)
        k_ref[:, j * LANES:(j + 1) * LANES] = blk.astype(BF16)
        if kind == "a":
            zk = fold_sq(zk, blk)
    if kind == "a":
        if zq is None:
            zq = jnp.zeros_like(zk)
        z = jnp.concatenate([zq, zk], axis=1).astype(BF16)
        sums = jnp.dot(z, sel_ref[...], preferred_element_type=F32)
        nrm_ref[...] = jnp.broadcast_to(jnp.max(sums, axis=0, keepdims=True), nrm_ref.shape)
    vlane = lax.broadcasted_iota(jnp.int32, (y.shape[0], LANES), 1)
    for j in range(KV_HEADS):
        vblk = y[:, q_w + kv_w + j * LANES:q_w + kv_w + (j + 1) * LANES]
        if kind == "a":
            v_ref[:, j * LANES:(j + 1) * LANES] = jnp.where(vlane == A_HEAD_DIM, 1.0, vblk).astype(BF16)
        else:
            v_ref[:, (2 * j) * LANES:(2 * j + 1) * LANES] = vblk.astype(BF16)
            v_ref[:, (2 * j + 1) * LANES:(2 * j + 2) * LANES] = jnp.where(vlane == 0, 1.0, 0.0).astype(BF16)


def _qkv_call(x2, mod3, mod_row, g_row, w, tables, gains, *, kind, want_q, tm, name):
    t_tokens, d = x2.shape
    n_w = w.shape[1]
    kv_w = KV_HEADS * LANES
    rope = tables is not None
    grid = (t_tokens // tm,)
    in_specs = [
        pl.BlockSpec((tm, d), lambda t: (t, 0)),
        pl.BlockSpec((None, 1, mod3.shape[2]), lambda t: (mod_row(t), 0, 0)),
        _const_spec((1, d)),
        _const_spec((d, n_w)),
    ]
    args = [x2, mod3, g_row, w]
    if rope:
        cos, sin = tables
        n_tab = cos.shape[0] // tm
        in_specs += [pl.BlockSpec((tm, LANES), lambda t: (t % n_tab, 0))] * 2
        args += [cos, sin]
    if kind == "c":
        in_specs += [_const_spec((1, LANES))] * 2
        args += list(gains)
    else:
        lane = jnp.arange(2 * LANES)[:, None]
        col = jnp.arange(LANES)[None, :]
        sel = (lane // A_HEAD_DIM == col) & (col < 3)
        in_specs.append(_const_spec((2 * LANES, LANES)))
        args.append(sel.astype(BF16))
    out_specs = []
    out_shape = []
    if want_q:
        out_specs.append(pl.BlockSpec((tm, d), lambda t: (t, 0)))
        out_shape.append(jax.ShapeDtypeStruct((t_tokens, d), BF16))
    for width in (kv_w, kv_w if kind == "a" else 2 * kv_w):
        out_specs.append(pl.BlockSpec((tm, width), lambda t: (t, 0)))
        out_shape.append(jax.ShapeDtypeStruct((t_tokens, width), BF16))
    if kind == "a":
        out_specs.append(pl.BlockSpec((SUBLANES, LANES), lambda t: (t, 0)))
        out_shape.append(jax.ShapeDtypeStruct((grid[0] * SUBLANES, LANES), F32))
    return pl.pallas_call(
        functools.partial(_qkv_kernel, kind=kind, rope=rope, want_q=want_q, d=d),
        grid=grid,
        in_specs=in_specs,
        out_specs=out_specs,
        out_shape=out_shape,
        compiler_params=_params(40),
        name=name,
    )(*args)


def _att_a_kernel(*refs, band, nb, row_max):
    if not row_max:
        shift_ref, refs = refs[0], refs[1:]
    if band:
        (sink_ref, q_ref, kp_ref, kc_ref, kn_ref, vp_ref, vc_ref, vn_ref,
         kx_ref, vx_ref, o_ref) = refs
    else:
        sink_ref, q_ref, kx_ref, vx_ref, o_ref = refs
    group = 4
    rows = group * BLOCK
    chunk = BLOCK // 2
    n_ctx = kx_ref.shape[0]
    lane = lax.broadcasted_iota(jnp.int32, (BLOCK, LANES), 1)
    low = lane < A_HEAD_DIM
    zero = jnp.zeros((BLOCK, LANES), BF16)
    if band:
        i = pl.program_id(1)
        row = lax.broadcasted_iota(jnp.int32, (chunk, BLOCK), 0)
        col = lax.broadcasted_iota(jnp.int32, (chunk, BLOCK), 1)

    for h in range(KV_HEADS):
        ks = slice(h * LANES, (h + 1) * LANES)
        if band:
            kb = jnp.concatenate([kp_ref[:, ks], kc_ref[:, ks], kn_ref[:, ks], kx_ref[:, ks]], axis=0)
            vb = jnp.concatenate([vp_ref[:, ks], vc_ref[:, ks], vn_ref[:, ks], vx_ref[:, ks]], axis=0)
        else:
            kb = kx_ref[:, ks]
            vb = vx_ref[:, ks]
        b0 = q_ref[:, (2 * h) * LANES:(2 * h + 1) * LANES]
        b1 = q_ref[:, (2 * h + 1) * LANES:(2 * h + 2) * LANES]
        lhs = jnp.concatenate(
            [jnp.where(low, b0, zero), jnp.where(low, zero, b0),
             jnp.where(low, b1, zero), jnp.where(low, zero, b1)], axis=0)
        s = lax.dot_general(lhs, kb, (((1,), (1,)), ((), ())), preferred_element_type=F32)
        es = []
        sink_terms = []
        for r0 in range(0, rows, chunk):
            sink = sink_ref[group * h + r0 // BLOCK] * LOG2E
            sc = s[r0:r0 + chunk]
            if band:
                p = row + (r0 % BLOCK)
                parts = [jnp.where((col >= p) & (i > 0), sc[:, 0:BLOCK], NEG_INF),
                         sc[:, BLOCK:2 * BLOCK],
                         jnp.where((col <= p) & (i < nb - 1), sc[:, 2 * BLOCK:3 * BLOCK], NEG_INF)]
                first_ctx = 3 * BLOCK
            else:
                parts = []
                first_ctx = 0
            parts += [sc[:, first_ctx + t * BLOCK:first_ctx + (t + 1) * BLOCK]
                      for t in range(n_ctx // BLOCK)]
            if row_max:
                m = functools.reduce(jnp.maximum, parts)
                m = jnp.maximum(jnp.max(m, axis=1, keepdims=True), sink)
            else:
                m = jnp.full((chunk, 1), jnp.maximum(shift_ref[0], sink), F32)
            es.append(jnp.concatenate([jnp.exp2(part - m).astype(BF16) for part in parts], axis=1))
            sink_terms.append(jnp.exp2(sink - m))
        pv = jnp.dot(jnp.concatenate(es, axis=0), vb, preferred_element_type=F32)
        l = pv[:, A_HEAD_DIM:A_HEAD_DIM + 1] + jnp.concatenate(sink_terms, axis=0)
        pv = pv / l
        o0 = jnp.where(low, pv[0:BLOCK], pltpu.roll(pv[BLOCK:2 * BLOCK], A_HEAD_DIM, axis=1))
        o1 = jnp.where(low, pv[2 * BLOCK:3 * BLOCK], pltpu.roll(pv[3 * BLOCK:4 * BLOCK], A_HEAD_DIM, axis=1))
        o_ref[:, (2 * h) * LANES:(2 * h + 1) * LANES] = o0.astype(BF16)
        o_ref[:, (2 * h + 1) * LANES:(2 * h + 2) * LANES] = o1.astype(BF16)


def _att_a_call(sink, q, k, v, kx, vx, *, bsz, band, name, shift=None):
    t_tokens, d = q.shape
    row_max = shift is None
    nb = t_tokens // bsz // BLOCK
    n_ctx = kx.shape[0] // bsz
    kv_w = KV_HEADS * LANES
    qspec = pl.BlockSpec((BLOCK, d), lambda b, i: (b * nb + i, 0))
    xspec = pl.BlockSpec((n_ctx, kv_w), lambda b, i: (b, 0))
    sspec = pl.BlockSpec(memory_space=pltpu.SMEM)
    if band:
        prev = pl.BlockSpec((BLOCK, kv_w), lambda b, i: (b * nb + jnp.maximum(i - 1, 0), 0))
        cur = pl.BlockSpec((BLOCK, kv_w), lambda b, i: (b * nb + i, 0))
        nxt = pl.BlockSpec((BLOCK, kv_w), lambda b, i: (b * nb + jnp.minimum(i + 1, nb - 1), 0))
        in_specs = [sspec, qspec, prev, cur, nxt, prev, cur, nxt, xspec, xspec]
        args = (sink, q, k, k, k, v, v, v, kx, vx)
    else:
        in_specs = [sspec, qspec, xspec, xspec]
        args = (sink, q, kx, vx)
    if not row_max:
        in_specs = [sspec] + in_specs
        args = (shift.reshape(1).astype(F32),) + args
    return pl.pallas_call(
        functools.partial(_att_a_kernel, band=band, nb=nb, row_max=row_max),
        grid=(bsz, nb),
        in_specs=in_specs,
        out_specs=qspec,
        out_shape=jax.ShapeDtypeStruct((t_tokens, d), BF16),
        compiler_params=_params(32, 2),
        name=name,
    )(*args)


def _att_c_kernel(*refs, lat, row_max):
    if not row_max:
        shift_ref, refs = refs[0], refs[1:]
    if lat:
        q_ref, k_ref, v_ref, kx_ref, vx_ref, o_ref = refs
    else:
        q_ref, kx_ref, vx_ref, o_ref = refs
    tq = q_ref.shape[0]
    nt = (((1,), (1,)), ((), ()))
    for h in range(KV_HEADS):
        ks = slice(h * LANES, (h + 1) * LANES)
        vs = slice(2 * h * LANES, (2 * h + 2) * LANES)
        lhs = jnp.concatenate(
            [q_ref[:, (2 * h) * LANES:(2 * h + 1) * LANES],
             q_ref[:, (2 * h + 1) * LANES:(2 * h + 2) * LANES]], axis=0)
        s_ctx = lax.dot_general(lhs, kx_ref[:, ks], nt, preferred_element_type=F32)
        if lat:
            s_lat = lax.dot_general(lhs, k_ref[:, ks], nt, preferred_element_type=F32)
        if row_max:
            m = jnp.max(s_ctx, axis=1, keepdims=True)
            if lat:
                m = jnp.maximum(m, jnp.max(s_lat, axis=1, keepdims=True))
        else:
            m = shift_ref[0]
        pv = jnp.dot(jnp.exp2(s_ctx - m).astype(BF16), vx_ref[:, vs], preferred_element_type=F32)
        if lat:
            pv = pv + jnp.dot(jnp.exp2(s_lat - m).astype(BF16), v_ref[:, vs],
                              preferred_element_type=F32)
        out = pv[:, :LANES] / pv[:, LANES:LANES + 1]
        o_ref[:, (2 * h) * LANES:(2 * h + 1) * LANES] = out[:tq].astype(BF16)
        o_ref[:, (2 * h + 1) * LANES:(2 * h + 2) * LANES] = out[tq:].astype(BF16)


def _att_c_call(q, k, v, kx, vx, *, bsz, lat, tq, name, shift=None):
    t_tokens, d = q.shape
    row_max = shift is None
    nq = t_tokens // bsz // tq
    n_ctx = kx.shape[0] // bsz
    kv_w = KV_HEADS * LANES
    qspec = pl.BlockSpec((tq, d), lambda b, i: (b * nq + i, 0))
    kxspec = pl.BlockSpec((n_ctx, kv_w), lambda b, i: (b, 0))
    vxspec = pl.BlockSpec((n_ctx, 2 * kv_w), lambda b, i: (b, 0))
    if lat:
        n_lat = k.shape[0] // bsz
        kspec = pl.BlockSpec((n_lat, kv_w), lambda b, i: (b, 0))
        vspec = pl.BlockSpec((n_lat, 2 * kv_w), lambda b, i: (b, 0))
        in_specs = [qspec, kspec, vspec, kxspec, vxspec]
        args = (q, k, v, kx, vx)
    else:
        in_specs = [qspec, kxspec, vxspec]
        args = (q, kx, vx)
    if not row_max:
        in_specs = [pl.BlockSpec(memory_space=pltpu.SMEM)] + in_specs
        args = (shift.reshape(1).astype(F32),) + args
    return pl.pallas_call(
        functools.partial(_att_c_kernel, lat=lat, row_max=row_max),
        grid=(bsz, nq),
        in_specs=in_specs,
        out_specs=qspec,
        out_shape=jax.ShapeDtypeStruct((t_tokens, d), BF16),
        compiler_params=_params(48, 2),
        name=name,
    )(*args)


def _gmlp_kernel(x_ref, mod_ref, g_ref, win_ref, bin_ref, lng_ref, lnb_ref, ws_ref, bs_ref,
                 a_ref, *, d):
    tm = x_ref.shape[0]
    shift = mod_ref[:, 0:d]
    scale = mod_ref[:, d:2 * d]
    h = _rms(x_ref[...], g_ref[...]) * (1.0 + scale) + shift
    z = jnp.dot(h.astype(BF16), win_ref[...], preferred_element_type=F32) + bin_ref[...]
    z = 0.5 * z * (1.0 + lax.erf(z * (2.0 ** -0.5)))
    width = z.shape[1] // 2
    u = z[:, :width]
    v = z[:, width:]
    mu = jnp.mean(v, axis=-1, keepdims=True)
    vc = v - mu
    var = jnp.mean(vc * vc, axis=-1, keepdims=True)
    vn = (vc * lax.rsqrt(var + LN_EPS) * lng_ref[...] + lnb_ref[...]).astype(BF16)
    n_chunk = tm // BLOCK
    for g in range(B_GROUPS):
        gs = slice(g * LANES, (g + 1) * LANES)
        rhs = jnp.concatenate([vn[c * BLOCK:(c + 1) * BLOCK, gs] for c in range(n_chunk)], axis=1)
        mixed = jnp.dot(ws_ref[g], rhs, preferred_element_type=F32)
        bias = bs_ref[g]
        for c in range(n_chunk):
            rs = slice(c * BLOCK, (c + 1) * BLOCK)
            a_ref[rs, gs] = (u[rs, gs] * (mixed[:, c * BLOCK:(c + 1) * BLOCK] + bias)).astype(BF16)


def _gmlp_call(x2, mod3, mod_row, g_row, w_in, b_in, ln_g, ln_b, w_s, b_s_tile, *, tm, name):
    t_tokens, d = x2.shape
    n_in = w_in.shape[1]
    width = n_in // 2
    return pl.pallas_call(
        functools.partial(_gmlp_kernel, d=d),
        grid=(t_tokens // tm,),
        in_specs=[
            pl.BlockSpec((tm, d), lambda t: (t, 0)),
            pl.BlockSpec((None, 1, mod3.shape[2]), lambda t: (mod_row(t), 0, 0)),
            _const_spec((1, d)),
            _const_spec((d, n_in)),
            _const_spec((1, n_in)),
            _const_spec((1, width)),
            _const_spec((1, width)),
            _const_spec(w_s.shape),
            _const_spec(b_s_tile.shape),
        ],
        out_specs=pl.BlockSpec((tm, width), lambda t: (t, 0)),
        out_shape=jax.ShapeDtypeStruct((t_tokens, width), BF16),
        compiler_params=_params(48),
        name=name,
    )(x2, mod3, g_row, w_in, b_in, ln_g, ln_b, w_s, b_s_tile)


def _post_ffn_kernel(a_ref, x_ref, mod_ref, ng_ref, wo_ref, win_ref, wout_ref, o_ref, *,
                     d, n_chunks):
    g_mix = mod_ref[:, 2 * d:3 * d]
    sh_f = mod_ref[:, 3 * d:4 * d]
    sc_f = mod_ref[:, 4 * d:5 * d]
    g_ffn = mod_ref[:, 5 * d:6 * d]
    y = jnp.dot(a_ref[...], wo_ref[...], preferred_element_type=F32)
    x1 = x_ref[...] + g_mix * _rms(y, ng_ref[1:2, :])
    h = (_rms(x1, ng_ref[2:3, :]) * (1.0 + sc_f) + sh_f).astype(BF16)
    hidden = wout_ref.shape[0]
    fc = hidden // n_chunks
    f = None
    for c in range(n_chunks):
        gate = jnp.dot(h, win_ref[:, c * fc:(c + 1) * fc], preferred_element_type=F32)
        up = jnp.dot(h, win_ref[:, hidden + c * fc:hidden + (c + 1) * fc],
                     preferred_element_type=F32)
        act = (_silu(gate) * up).astype(BF16)
        part = jnp.dot(act, wout_ref[c * fc:(c + 1) * fc, :], preferred_element_type=F32)
        f = part if f is None else f + part
    o_ref[...] = x1 + g_ffn * _rms(f, ng_ref[3:4, :])


def _post_ffn_call(a2, x2, mod3, mod_row, ng, w_o, w_in, w_out, *, tm, n_chunks, name):
    t_tokens, d = x2.shape
    return pl.pallas_call(
        functools.partial(_post_ffn_kernel, d=d, n_chunks=n_chunks),
        grid=(t_tokens // tm,),
        in_specs=[
            pl.BlockSpec((tm, a2.shape[1]), lambda t: (t, 0)),
            pl.BlockSpec((tm, d), lambda t: (t, 0)),
            pl.BlockSpec((None, 1, mod3.shape[2]), lambda t: (mod_row(t), 0, 0)),
            _const_spec(ng.shape),
            _const_spec(w_o.shape),
            _const_spec(w_in.shape),
            _const_spec(w_out.shape),
        ],
        out_specs=pl.BlockSpec((tm, d), lambda t: (t, 0)),
        out_shape=jax.ShapeDtypeStruct((t_tokens, d), F32),
        compiler_params=_params(56),
        name=name,
    )(a2, x2, mod3, ng, w_o, w_in, w_out)


def _rope_tables(n_tokens, head_dim):
    rows = n_tokens // GRID_W
    row_pos = jnp.repeat(jnp.arange(rows, dtype=F32), GRID_W)
    col_pos = jnp.tile(jnp.arange(GRID_W, dtype=F32), rows)
    n_freq = head_dim // 4
    inv_freq = ROPE_THETA ** (-jnp.arange(n_freq, dtype=F32) / n_freq)
    angles = jnp.concatenate([row_pos[:, None] * inv_freq, col_pos[:, None] * inv_freq], axis=-1)
    cos = jnp.cos(angles)
    sin = jnp.sin(angles)
    reps = LANES // head_dim
    cos_l = jnp.tile(jnp.concatenate([cos, cos], axis=-1), (1, reps))
    sin_l = jnp.tile(jnp.concatenate([-sin, sin], axis=-1), (1, reps))
    return cos_l, sin_l


def _dup_heads(w, head_dim):
    d, n = w.shape
    reps = LANES // head_dim
    w = w.reshape(d, n // head_dim, 1, head_dim)
    return jnp.broadcast_to(w, (d, n // head_dim, reps, head_dim)).reshape(d, n * reps)


def _pad_heads(w, head_dim):
    d, n = w.shape
    w = w.reshape(d, n // head_dim, head_dim)
    return jnp.pad(w, ((0, 0), (0, 0), (0, LANES - head_dim))).reshape(d, n // head_dim * LANES)


def kernel(x, c, ctx, c_ctx, ada_w, ada_b, norm_g, ffn_w_in, ffn_w_out, a_w_qkv, a_w_o, a_sink,
           b_w_in, b_b_in, b_ln_g, b_ln_b, b_w_s, b_b_s, b_w_o, c_w_qkv, c_w_o, c_q_g, c_k_g):
    bsz, n, d = x.shape
    n_ctx = ctx.shape[1]
    depth = ada_w.shape[0]
    assert bsz < MOD_ROWS and n % 512 == 0 and n_ctx % 256 == 0 and d % LANES == 0

    tm_lat = 512
    tm_ctx = 512 if (bsz * n_ctx) % 512 == 0 else 256
    lat_tiles = n // tm_lat
    ctx_row = bsz

    c_rows = jnp.zeros((MOD_ROWS, d), F32).at[:bsz].set(c).at[ctx_row].set(c_ctx)
    mod = _ada_call(c_rows, ada_w, ada_b)
    mod3 = mod.reshape(depth * MOD_ROWS, 1, 6 * d)

    tab_a = _rope_tables(n, A_HEAD_DIM)
    tab_c = _rope_tables(n, C_HEAD_DIM)

    xs = x.reshape(bsz * n, d)
    cs = ctx.reshape(bsz * n_ctx, d)
    q_w = d

    for i in range(depth):
        ctx_out = i < depth - 1
        kind = i % N_MIXERS
        j = i // N_MIXERS
        ng = norm_g[i]
        g0 = ng[0:1]

        def lat_row(t, i=i):
            return i * MOD_ROWS + t // lat_tiles

        def ctx_row_fn(t, i=i):
            return i * MOD_ROWS + ctx_row

        if kind == 0:
            w = a_w_qkv[j]
            kv_cols = KV_HEADS * A_HEAD_DIM
            w_all = jnp.concatenate(
                [w[:, :q_w], _dup_heads(w[:, q_w:q_w + kv_cols], A_HEAD_DIM),
                 _pad_heads(w[:, q_w + kv_cols:], A_HEAD_DIM)], axis=1).astype(BF16)
            q, k, v, nrm = _qkv_call(xs, mod3, lat_row, g0, w_all, tab_a, None,
                                     kind="a", want_q=True, tm=tm_lat, name=f"qkv_a_lat{i}")
            if ctx_out:
                qx, kx, vx, nrm_x = _qkv_call(cs, mod3, ctx_row_fn, g0, w_all, None, None,
                                              kind="a", want_q=True, tm=tm_ctx,
                                              name=f"qkv_a_ctx{i}")
            else:
                kx, vx, nrm_x = _qkv_call(cs, mod3, ctx_row_fn, g0, w_all[:, q_w:], None, None,
                                          kind="a", want_q=False, tm=tm_ctx, name=f"kv_a_ctx{i}")
            sink = a_sink[j]
            nrm_all = jnp.maximum(jnp.max(nrm, axis=0), jnp.max(nrm_x, axis=0))
            bound = 1.03 * jnp.sqrt(jnp.maximum(nrm_all[0], nrm_all[1]) * nrm_all[2])
            use_bound = bound - jnp.min(sink) * LOG2E <= A_SHIFT_HEADROOM

            def att_a(qq, band, tag):
                kk, vv = (k, v) if band else (None, None)
                return lax.cond(
                    use_bound,
                    lambda: _att_a_call(sink, qq, kk, vv, kx, vx, bsz=bsz, band=band, shift=bound,
                                        name=f"att_a_{tag}_bounded{i}"),
                    lambda: _att_a_call(sink, qq, kk, vv, kx, vx, bsz=bsz, band=band,
                                        name=f"att_a_{tag}{i}"))

            a_lat = att_a(q, True, "lat")
            if ctx_out:
                a_ctx = att_a(qx, False, "ctx")
            w_o = a_w_o[j]
        elif kind == 1:
            w_in = b_w_in[j].astype(BF16)
            b_in = b_b_in[j][None, :]
            ln_g = b_ln_g[j][None, :]
            ln_b = b_ln_b[j][None, :]
            w_s = b_w_s[j].astype(BF16)
            bs_tile = jnp.broadcast_to(b_b_s[j][:, :, None], b_b_s[j].shape + (LANES,))
            a_lat = _gmlp_call(xs, mod3, lat_row, g0, w_in, b_in, ln_g, ln_b, w_s, bs_tile,
                               tm=tm_lat, name=f"gmlp_lat{i}")
            if ctx_out:
                a_ctx = _gmlp_call(cs, mod3, ctx_row_fn, g0, w_in, b_in, ln_g, ln_b, w_s, bs_tile,
                                   tm=tm_ctx, name=f"gmlp_ctx{i}")
            w_o = b_w_o[j]
        else:
            w_all = c_w_qkv[j].astype(BF16)
            gains = (c_q_g[j][None, :], c_k_g[j][None, :])
            q, k, v = _qkv_call(xs, mod3, lat_row, g0, w_all, tab_c, gains,
                                kind="c", want_q=True, tm=tm_lat, name=f"qkv_c_lat{i}")
            if ctx_out:
                qx, kx, vx = _qkv_call(cs, mod3, ctx_row_fn, g0, w_all, None, gains,
                                       kind="c", want_q=True, tm=tm_ctx, name=f"qkv_c_ctx{i}")
            else:
                kx, vx = _qkv_call(cs, mod3, ctx_row_fn, g0, w_all[:, q_w:], None, gains,
                                   kind="c", want_q=False, tm=tm_ctx, name=f"kv_c_ctx{i}")
            bound = (C_LOGIT_BOUND * jnp.max(jnp.abs(c_q_g[j])) * jnp.max(jnp.abs(c_k_g[j]))).astype(F32)
            a_lat = lax.cond(
                2.0 * bound < F32_SAFE_EXP2_RANGE,
                lambda: _att_c_call(q, k, v, kx, vx, bsz=bsz, lat=True, tq=256, shift=bound,
                                    name=f"att_c_lat_bounded{i}"),
                lambda: _att_c_call(q, k, v, kx, vx, bsz=bsz, lat=True, tq=256,
                                    name=f"att_c_lat{i}"))
            if ctx_out:
                a_ctx = _att_c_call(qx, None, None, kx, vx, bsz=bsz, lat=False, tq=n_ctx,
                                    name=f"att_c_ctx{i}")
            w_o = c_w_o[j]

        w_o = w_o.astype(BF16)
        f_in = ffn_w_in[i].astype(BF16)
        f_out = ffn_w_out[i].astype(BF16)
        n_chunks = f_out.shape[0] // MXU_TILE
        xs = _post_ffn_call(a_lat, xs, mod3, lat_row, ng, w_o, f_in, f_out,
                            tm=tm_lat, n_chunks=n_chunks, name=f"post_ffn_lat{i}")
        if ctx_out:
            cs = _post_ffn_call(a_ctx, cs, mod3, ctx_row_fn, ng, w_o, f_in, f_out,
                                tm=tm_ctx, n_chunks=n_chunks, name=f"post_ffn_ctx{i}")
    return xs.reshape(bsz, n, d)
```

```python
import functools
import math

import jax
import jax.numpy as jnp
from jax import lax
from jax.experimental import pallas as pl
from jax.experimental.pallas import tpu as pltpu

F32 = jnp.float32
BF16 = jnp.bfloat16

LANES = 128
SUBLANES = 8
MXU_TILE = 256
MIB = 1024 * 1024

GRID_W = 64
N_MIXERS = 3
BLOCK = 128
A_HEAD_DIM = 64
C_HEAD_DIM = 128
KV_HEADS = 4
B_GROUPS = 8
ROPE_THETA = 10000.0
RMS_EPS = 1e-6
LN_EPS = 1e-5
NEG_INF = -1e30
LOG2E = math.log2(math.e)
C_LOGIT_BOUND = 1.02 * C_HEAD_DIM * C_HEAD_DIM ** -0.5 * LOG2E
F32_SAFE_EXP2_RANGE = 120.0
A_SHIFT_HEADROOM = 100.0
POST_SUB_ROWS = 512
MOD_ROWS = 40


def _rms(x, g):
    return x * lax.rsqrt(jnp.mean(x * x, axis=-1, keepdims=True) + RMS_EPS) * g


def _silu(x):
    return x / (1.0 + jnp.exp(-x))


def _const_spec(shape):
    nd = len(shape)
    return pl.BlockSpec(shape, lambda *_: (0,) * nd, pipeline_mode=pl.Buffered(1))


def _params(vmem_mib, n_axes=1):
    return pltpu.CompilerParams(
        dimension_semantics=("parallel",) * n_axes,
        vmem_limit_bytes=vmem_mib * MIB,
    )


def _ada_kernel(c_ref, w_ref, b_ref, o_ref):
    a = _silu(c_ref[...]).astype(BF16)
    o_ref[...] = jnp.dot(a, w_ref[...].astype(BF16), preferred_element_type=F32) + b_ref[...]


def _ada_call(c_rows, ada_w, ada_b):
    depth, d, n6 = ada_w.shape
    tn = 1536
    return pl.pallas_call(
        _ada_kernel,
        grid=(depth, n6 // tn),
        in_specs=[
            pl.BlockSpec((MOD_ROWS, d), lambda i, j: (0, 0)),
            pl.BlockSpec((None, d, tn), lambda i, j: (i, 0, j)),
            pl.BlockSpec((None, 1, tn), lambda i, j: (i, 0, j)),
        ],
        out_specs=pl.BlockSpec((None, MOD_ROWS, tn), lambda i, j: (i, 0, j)),
        out_shape=jax.ShapeDtypeStruct((depth, MOD_ROWS, n6), F32),
        compiler_params=_params(40, 2),
        name="ada_mod",
    )(c_rows, ada_w, ada_b.reshape(depth, 1, n6))


def _qkv_kernel(*refs, kind, rope, want_q, d):
    it = iter(refs)
    x_ref, mod_ref, g_ref, w_ref = next(it), next(it), next(it), next(it)
    cos_ref = sin_ref = qg_ref = kg_ref = q_ref = None
    if rope:
        cos_ref, sin_ref = next(it), next(it)
    if kind == "c":
        qg_ref, kg_ref = next(it), next(it)
    else:
        sel_ref = next(it)
    if want_q:
        q_ref = next(it)
    k_ref, v_ref = next(it), next(it)
    if kind == "a":
        nrm_ref = next(it)

    shift = mod_ref[:, 0:d]
    scale = mod_ref[:, d:2 * d]
    h = _rms(x_ref[...], g_ref[...]) * (1.0 + scale) + shift
    y = jnp.dot(h.astype(BF16), w_ref[...], preferred_element_type=F32)

    q_w = d if want_q else 0
    kv_w = KV_HEADS * LANES
    if rope:
        cos = cos_ref[...]
        sin = sin_ref[...]
        lane = lax.broadcasted_iota(jnp.int32, cos.shape, 1)
        first_half = (lane & (A_HEAD_DIM - 1)) < (A_HEAD_DIM // 2)

    def rotate(blk):
        if kind == "a":
            fwd = pltpu.roll(blk, LANES - A_HEAD_DIM // 2, axis=1)
            bwd = pltpu.roll(blk, A_HEAD_DIM // 2, axis=1)
            return jnp.where(first_half, fwd, bwd)
        return pltpu.roll(blk, C_HEAD_DIM // 2, axis=1)

    def head_block(col, gain_ref, out_scale):
        blk = y[:, col:col + LANES]
        if kind == "c":
            blk = _rms(blk, gain_ref[...])
        if rope:
            blk = blk * cos + rotate(blk) * sin
        if out_scale != 1.0:
            blk = blk * out_scale
        return blk

    def fold_sq(acc, blk):
        sq = blk * blk
        return sq if acc is None else jnp.maximum(acc, sq)

    head_dim = A_HEAD_DIM if kind == "a" else C_HEAD_DIM
    q_scale = head_dim ** -0.5 * LOG2E
    zq = zk = None
    for j in range(q_w // LANES):
        blk = head_block(j * LANES, qg_ref, q_scale)
        q_ref[:, j * LANES:(j + 1) * LANES] = blk.astype(BF16)
        if kind == "a":
            zq = fold_sq(zq, blk)
    for j in range(KV_HEADS):
        blk = head_block(q_w + j * LANES, kg_ref, 1.0)
        k_ref[:, j * LANES:(j + 1) * LANES] = blk.astype(BF16)
        if kind == "a":
            zk = fold_sq(zk, blk)
    if kind == "a":
        if zq is None:
            zq = jnp.zeros_like(zk)
        z = jnp.concatenate([zq, zk], axis=1).astype(BF16)
        sums = jnp.dot(z, sel_ref[...], preferred_element_type=F32)
        nrm_ref[...] = jnp.broadcast_to(jnp.max(sums, axis=0, keepdims=True), nrm_ref.shape)
    vlane = lax.broadcasted_iota(jnp.int32, (y.shape[0], LANES), 1)
    for j in range(KV_HEADS):
        vblk = y[:, q_w + kv_w + j * LANES:q_w + kv_w + (j + 1) * LANES]
        if kind == "a":
            v_ref[:, j * LANES:(j + 1) * LANES] = jnp.where(vlane == A_HEAD_DIM, 1.0, vblk).astype(BF16)
        else:
            v_ref[:, (2 * j) * LANES:(2 * j + 1) * LANES] = vblk.astype(BF16)
            v_ref[:, (2 * j + 1) * LANES:(2 * j + 2) * LANES] = jnp.where(vlane == 0, 1.0, 0.0).astype(BF16)


def _qkv_call(x2, mod3, mod_row, g_row, w, tables, gains, *, kind, want_q, tm, name):
    t_tokens, d = x2.shape
    n_w = w.shape[1]
    kv_w = KV_HEADS * LANES
    rope = tables is not None
    grid = (t_tokens // tm,)
    in_specs = [
        pl.BlockSpec((tm, d), lambda t: (t, 0)),
        pl.BlockSpec((None, 1, mod3.shape[2]), lambda t: (mod_row(t), 0, 0)),
        _const_spec((1, d)),
        _const_spec((d, n_w)),
    ]
    args = [x2, mod3, g_row, w]
    if rope:
        cos, sin = tables
        n_tab = cos.shape[0] // tm
        in_specs += [pl.BlockSpec((tm, LANES), lambda t: (t % n_tab, 0))] * 2
        args += [cos, sin]
    if kind == "c":
        in_specs += [_const_spec((1, LANES))] * 2
        args += list(gains)
    else:
        lane = jnp.arange(2 * LANES)[:, None]
        col = jnp.arange(LANES)[None, :]
        sel = (lane // A_HEAD_DIM == col) & (col < 3)
        in_specs.append(_const_spec((2 * LANES, LANES)))
        args.append(sel.astype(BF16))
    out_specs = []
    out_shape = []
    if want_q:
        out_specs.append(pl.BlockSpec((tm, d), lambda t: (t, 0)))
        out_shape.append(jax.ShapeDtypeStruct((t_tokens, d), BF16))
    for width in (kv_w, kv_w if kind == "a" else 2 * kv_w):
        out_specs.append(pl.BlockSpec((tm, width), lambda t: (t, 0)))
        out_shape.append(jax.ShapeDtypeStruct((t_tokens, width), BF16))
    if kind == "a":
        out_specs.append(pl.BlockSpec((SUBLANES, LANES), lambda t: (t, 0)))
        out_shape.append(jax.ShapeDtypeStruct((grid[0] * SUBLANES, LANES), F32))
    return pl.pallas_call(
        functools.partial(_qkv_kernel, kind=kind, rope=rope, want_q=want_q, d=d),
        grid=grid,
        in_specs=in_specs,
        out_specs=out_specs,
        out_shape=out_shape,
        compiler_params=_params(40),
        name=name,
    )(*args)


def _att_a_kernel(*refs, band, nb, row_max):
    if not row_max:
        shift_ref, refs = refs[0], refs[1:]
    if band:
        (sink_ref, q_ref, kp_ref, kc_ref, kn_ref, vp_ref, vc_ref, vn_ref,
         kx_ref, vx_ref, o_ref) = refs
    else:
        sink_ref, q_ref, kx_ref, vx_ref, o_ref = refs
    group = 4
    rows = group * BLOCK
    chunk = BLOCK // 2
    n_ctx = kx_ref.shape[0]
    lane = lax.broadcasted_iota(jnp.int32, (BLOCK, LANES), 1)
    low = lane < A_HEAD_DIM
    zero = jnp.zeros((BLOCK, LANES), BF16)
    if band:
        i = pl.program_id(1)
        row = lax.broadcasted_iota(jnp.int32, (chunk, BLOCK), 0)
        col = lax.broadcasted_iota(jnp.int32, (chunk, BLOCK), 1)

    for h in range(KV_HEADS):
        ks = slice(h * LANES, (h + 1) * LANES)
        if band:
            kb = jnp.concatenate([kp_ref[:, ks], kc_ref[:, ks], kn_ref[:, ks], kx_ref[:, ks]], axis=0)
            vb = jnp.concatenate([vp_ref[:, ks], vc_ref[:, ks], vn_ref[:, ks], vx_ref[:, ks]], axis=0)
        else:
            kb = kx_ref[:, ks]
            vb = vx_ref[:, ks]
        b0 = q_ref[:, (2 * h) * LANES:(2 * h + 1) * LANES]
        b1 = q_ref[:, (2 * h + 1) * LANES:(2 * h + 2) * LANES]
        lhs = jnp.concatenate(
            [jnp.where(low, b0, zero), jnp.where(low, zero, b0),
             jnp.where(low, b1, zero), jnp.where(low, zero, b1)], axis=0)
        s = lax.dot_general(lhs, kb, (((1,), (1,)), ((), ())), preferred_element_type=F32)
        es = []
        sink_terms = []
        for r0 in range(0, rows, chunk):
            sink = sink_ref[group * h + r0 // BLOCK] * LOG2E
            sc = s[r0:r0 + chunk]
            if band:
                p = row + (r0 % BLOCK)
                parts = [jnp.where((col >= p) & (i > 0), sc[:, 0:BLOCK], NEG_INF),
                         sc[:, BLOCK:2 * BLOCK],
                         jnp.where((col <= p) & (i < nb - 1), sc[:, 2 * BLOCK:3 * BLOCK], NEG_INF)]
                first_ctx = 3 * BLOCK
            else:
                parts = []
                first_ctx = 0
            parts += [sc[:, first_ctx + t * BLOCK:first_ctx + (t + 1) * BLOCK]
                      for t in range(n_ctx // BLOCK)]
            if row_max:
                m = functools.reduce(jnp.maximum, parts)
                m = jnp.maximum(jnp.max(m, axis=1, keepdims=True), sink)
            else:
                m = jnp.full((chunk, 1), jnp.maximum(shift_ref[0], sink), F32)
            es.append(jnp.concatenate([jnp.exp2(part - m).astype(BF16) for part in parts], axis=1))
            sink_terms.append(jnp.exp2(sink - m))
        pv = jnp.dot(jnp.concatenate(es, axis=0), vb, preferred_element_type=F32)
        l = pv[:, A_HEAD_DIM:A_HEAD_DIM + 1] + jnp.concatenate(sink_terms, axis=0)
        pv = pv / l
        o0 = jnp.where(low, pv[0:BLOCK], pltpu.roll(pv[BLOCK:2 * BLOCK], A_HEAD_DIM, axis=1))
        o1 = jnp.where(low, pv[2 * BLOCK:3 * BLOCK], pltpu.roll(pv[3 * BLOCK:4 * BLOCK], A_HEAD_DIM, axis=1))
        o_ref[:, (2 * h) * LANES:(2 * h + 1) * LANES] = o0.astype(BF16)
        o_ref[:, (2 * h + 1) * LANES:(2 * h + 2) * LANES] = o1.astype(BF16)


def _att_a_call(sink, q, k, v, kx, vx, *, bsz, band, name, shift=None):
    t_tokens, d = q.shape
    row_max = shift is None
    nb = t_tokens // bsz // BLOCK
    n_ctx = kx.shape[0] // bsz
    kv_w = KV_HEADS * LANES
    qspec = pl.BlockSpec((BLOCK, d), lambda b, i: (b * nb + i, 0))
    xspec = pl.BlockSpec((n_ctx, kv_w), lambda b, i: (b, 0))
    sspec = pl.BlockSpec(memory_space=pltpu.SMEM)
    if band:
        prev = pl.BlockSpec((BLOCK, kv_w), lambda b, i: (b * nb + jnp.maximum(i - 1, 0), 0))
        cur = pl.BlockSpec((BLOCK, kv_w), lambda b, i: (b * nb + i, 0))
        nxt = pl.BlockSpec((BLOCK, kv_w), lambda b, i: (b * nb + jnp.minimum(i + 1, nb - 1), 0))
        in_specs = [sspec, qspec, prev, cur, nxt, prev, cur, nxt, xspec, xspec]
        args = (sink, q, k, k, k, v, v, v, kx, vx)
    else:
        in_specs = [sspec, qspec, xspec, xspec]
        args = (sink, q, kx, vx)
    if not row_max:
        in_specs = [sspec] + in_specs
        args = (shift.reshape(1).astype(F32),) + args
    return pl.pallas_call(
        functools.partial(_att_a_kernel, band=band, nb=nb, row_max=row_max),
        grid=(bsz, nb),
        in_specs=in_specs,
        out_specs=qspec,
        out_shape=jax.ShapeDtypeStruct((t_tokens, d), BF16),
        compiler_params=_params(32, 2),
        name=name,
    )(*args)


def _att_c_kernel(*refs, lat, row_max):
    if not row_max:
        shift_ref, refs = refs[0], refs[1:]
    if lat:
        q_ref, k_ref, v_ref, kx_ref, vx_ref, o_ref = refs
    else:
        q_ref, kx_ref, vx_ref, o_ref = refs
    tq = q_ref.shape[0]
    nt = (((1,), (1,)), ((), ()))
    for h in range(KV_HEADS):
        ks = slice(h * LANES, (h + 1) * LANES)
        vs = slice(2 * h * LANES, (2 * h + 2) * LANES)
        lhs = jnp.concatenate(
            [q_ref[:, (2 * h) * LANES:(2 * h + 1) * LANES],
             q_ref[:, (2 * h + 1) * LANES:(2 * h + 2) * LANES]], axis=0)
        s_ctx = lax.dot_general(lhs, kx_ref[:, ks], nt, preferred_element_type=F32)
        if lat:
            s_lat = lax.dot_general(lhs, k_ref[:, ks], nt, preferred_element_type=F32)
        if row_max:
            m = jnp.max(s_ctx, axis=1, keepdims=True)
            if lat:
                m = jnp.maximum(m, jnp.max(s_lat, axis=1, keepdims=True))
        else:
            m = shift_ref[0]
        pv = jnp.dot(jnp.exp2(s_ctx - m).astype(BF16), vx_ref[:, vs], preferred_element_type=F32)
        if lat:
            pv = pv + jnp.dot(jnp.exp2(s_lat - m).astype(BF16), v_ref[:, vs],
                              preferred_element_type=F32)
        out = pv[:, :LANES] / pv[:, LANES:LANES + 1]
        o_ref[:, (2 * h) * LANES:(2 * h + 1) * LANES] = out[:tq].astype(BF16)
        o_ref[:, (2 * h + 1) * LANES:(2 * h + 2) * LANES] = out[tq:].astype(BF16)


def _att_c_call(q, k, v, kx, vx, *, bsz, lat, tq, name, shift=None):
    t_tokens, d = q.shape
    row_max = shift is None
    nq = t_tokens // bsz // tq
    n_ctx = kx.shape[0] // bsz
    kv_w = KV_HEADS * LANES
    qspec = pl.BlockSpec((tq, d), lambda b, i: (b * nq + i, 0))
    kxspec = pl.BlockSpec((n_ctx, kv_w), lambda b, i: (b, 0))
    vxspec = pl.BlockSpec((n_ctx, 2 * kv_w), lambda b, i: (b, 0))
    if lat:
        n_lat = k.shape[0] // bsz
        kspec = pl.BlockSpec((n_lat, kv_w), lambda b, i: (b, 0))
        vspec = pl.BlockSpec((n_lat, 2 * kv_w), lambda b, i: (b, 0))
        in_specs = [qspec, kspec, vspec, kxspec, vxspec]
        args = (q, k, v, kx, vx)
    else:
        in_specs = [qspec, kxspec, vxspec]
        args = (q, kx, vx)
    if not row_max:
        in_specs = [pl.BlockSpec(memory_space=pltpu.SMEM)] + in_specs
        args = (shift.reshape(1).astype(F32),) + args
    return pl.pallas_call(
        functools.partial(_att_c_kernel, lat=lat, row_max=row_max),
        grid=(bsz, nq),
        in_specs=in_specs,
        out_specs=qspec,
        out_shape=jax.ShapeDtypeStruct((t_tokens, d), BF16),
        compiler_params=_params(48, 2),
        name=name,
    )(*args)


def _gmlp_kernel(x_ref, mod_ref, g_ref, win_ref, bin_ref, lng_ref, lnb_ref, ws_ref, bs_ref,
                 a_ref, *, d):
    tm = x_ref.shape[0]
    shift = mod_ref[:, 0:d]
    scale = mod_ref[:, d:2 * d]
    h = _rms(x_ref[...], g_ref[...]) * (1.0 + scale) + shift
    z = jnp.dot(h.astype(BF16), win_ref[...], preferred_element_type=F32) + bin_ref[...]
    z = 0.5 * z * (1.0 + lax.erf(z * (2.0 ** -0.5)))
    width = z.shape[1] // 2
    u = z[:, :width]
    v = z[:, width:]
    mu = jnp.mean(v, axis=-1, keepdims=True)
    vc = v - mu
    var = jnp.mean(vc * vc, axis=-1, keepdims=True)
    vn = (vc * lax.rsqrt(var + LN_EPS) * lng_ref[...] + lnb_ref[...]).astype(BF16)
    n_chunk = tm // BLOCK
    for g in range(B_GROUPS):
        gs = slice(g * LANES, (g + 1) * LANES)
        rhs = jnp.concatenate([vn[c * BLOCK:(c + 1) * BLOCK, gs] for c in range(n_chunk)], axis=1)
        mixed = jnp.dot(ws_ref[g], rhs, preferred_element_type=F32)
        bias = bs_ref[g]
        for c in range(n_chunk):
            rs = slice(c * BLOCK, (c + 1) * BLOCK)
            a_ref[rs, gs] = (u[rs, gs] * (mixed[:, c * BLOCK:(c + 1) * BLOCK] + bias)).astype(BF16)


def _gmlp_call(x2, mod3, mod_row, g_row, w_in, b_in, ln_g, ln_b, w_s, b_s_tile, *, tm, name):
    t_tokens, d = x2.shape
    n_in = w_in.shape[1]
    width = n_in // 2
    return pl.pallas_call(
        functools.partial(_gmlp_kernel, d=d),
        grid=(t_tokens // tm,),
        in_specs=[
            pl.BlockSpec((tm, d), lambda t: (t, 0)),
            pl.BlockSpec((None, 1, mod3.shape[2]), lambda t: (mod_row(t), 0, 0)),
            _const_spec((1, d)),
            _const_spec((d, n_in)),
            _const_spec((1, n_in)),
            _const_spec((1, width)),
            _const_spec((1, width)),
            _const_spec(w_s.shape),
            _const_spec(b_s_tile.shape),
        ],
        out_specs=pl.BlockSpec((tm, width), lambda t: (t, 0)),
        out_shape=jax.ShapeDtypeStruct((t_tokens, width), BF16),
        compiler_params=_params(48),
        name=name,
    )(x2, mod3, g_row, w_in, b_in, ln_g, ln_b, w_s, b_s_tile)


def _post_ffn_kernel(a_ref, x_ref, mod_ref, ng_ref, wo_ref, win_ref, wout_ref, o_ref, *,
                     d, n_chunks):
    g_mix = mod_ref[:, 2 * d:3 * d]
    sh_f = mod_ref[:, 3 * d:4 * d]
    sc_f = mod_ref[:, 4 * d:5 * d]
    g_ffn = mod_ref[:, 5 * d:6 * d]
    hidden = wout_ref.shape[0]
    fc = hidden // n_chunks
    tm = x_ref.shape[0]
    sub = min(tm, POST_SUB_ROWS)
    for r0 in range(0, tm, sub):
        rs = slice(r0, r0 + sub)
        y = jnp.dot(a_ref[rs, :], wo_ref[...], preferred_element_type=F32)
        x1 = x_ref[rs, :] + g_mix * _rms(y, ng_ref[1:2, :])
        h = (_rms(x1, ng_ref[2:3, :]) * (1.0 + sc_f) + sh_f).astype(BF16)
        f = None
        for c in range(n_chunks):
            gate = jnp.dot(h, win_ref[:, c * fc:(c + 1) * fc], preferred_element_type=F32)
            up = jnp.dot(h, win_ref[:, hidden + c * fc:hidden + (c + 1) * fc],
                         preferred_element_type=F32)
            act = (_silu(gate) * up).astype(BF16)
            part = jnp.dot(act, wout_ref[c * fc:(c + 1) * fc, :], preferred_element_type=F32)
            f = part if f is None else f + part
        o_ref[rs, :] = x1 + g_ffn * _rms(f, ng_ref[3:4, :])


def _post_ffn_call(a2, x2, mod3, mod_row, ng, w_o, w_in, w_out, *, tm, n_chunks, name):
    t_tokens, d = x2.shape
    return pl.pallas_call(
        functools.partial(_post_ffn_kernel, d=d, n_chunks=n_chunks),
        grid=(t_tokens // tm,),
        in_specs=[
            pl.BlockSpec((tm, a2.shape[1]), lambda t: (t, 0)),
            pl.BlockSpec((tm, d), lambda t: (t, 0)),
            pl.BlockSpec((None, 1, mod3.shape[2]), lambda t: (mod_row(t), 0, 0)),
            _const_spec(ng.shape),
            _const_spec(w_o.shape),
            _const_spec(w_in.shape),
            _const_spec(w_out.shape),
        ],
        out_specs=pl.BlockSpec((tm, d), lambda t: (t, 0)),
        out_shape=jax.ShapeDtypeStruct((t_tokens, d), F32),
        compiler_params=_params(56),
        name=name,
    )(a2, x2, mod3, ng, w_o, w_in, w_out)


def _rope_tables(n_tokens, head_dim):
    rows = n_tokens // GRID_W
    row_pos = jnp.repeat(jnp.arange(rows, dtype=F32), GRID_W)
    col_pos = jnp.tile(jnp.arange(GRID_W, dtype=F32), rows)
    n_freq = head_dim // 4
    inv_freq = ROPE_THETA ** (-jnp.arange(n_freq, dtype=F32) / n_freq)
    angles = jnp.concatenate([row_pos[:, None] * inv_freq, col_pos[:, None] * inv_freq], axis=-1)
    cos = jnp.cos(angles)
    sin = jnp.sin(angles)
    reps = LANES // head_dim
    cos_l = jnp.tile(jnp.concatenate([cos, cos], axis=-1), (1, reps))
    sin_l = jnp.tile(jnp.concatenate([-sin, sin], axis=-1), (1, reps))
    return cos_l, sin_l


def _dup_heads(w, head_dim):
    d, n = w.shape
    reps = LANES // head_dim
    w = w.reshape(d, n // head_dim, 1, head_dim)
    return jnp.broadcast_to(w, (d, n // head_dim, reps, head_dim)).reshape(d, n * reps)


def _pad_heads(w, head_dim):
    d, n = w.shape
    w = w.reshape(d, n // head_dim, head_dim)
    return jnp.pad(w, ((0, 0), (0, 0), (0, LANES - head_dim))).reshape(d, n // head_dim * LANES)


def kernel(x, c, ctx, c_ctx, ada_w, ada_b, norm_g, ffn_w_in, ffn_w_out, a_w_qkv, a_w_o, a_sink,
           b_w_in, b_b_in, b_ln_g, b_ln_b, b_w_s, b_b_s, b_w_o, c_w_qkv, c_w_o, c_q_g, c_k_g):
    bsz, n, d = x.shape
    n_ctx = ctx.shape[1]
    depth = ada_w.shape[0]
    assert bsz < MOD_ROWS and n % 512 == 0 and n_ctx % 256 == 0 and d % LANES == 0

    tm_lat = 512
    tm_qkv_c = 256
    tm_post = 1024 if n % 1024 == 0 else 512
    tq_c = 512
    tm_ctx = 512 if (bsz * n_ctx) % 512 == 0 else 256
    tm_post_ctx = 1024 if (bsz * n_ctx) % 1024 == 0 else tm_ctx
    ctx_row = bsz

    c_rows = jnp.zeros((MOD_ROWS, d), F32).at[:bsz].set(c).at[ctx_row].set(c_ctx)
    mod = _ada_call(c_rows, ada_w, ada_b)
    mod3 = mod.reshape(depth * MOD_ROWS, 1, 6 * d)

    tab_a = _rope_tables(n, A_HEAD_DIM)
    tab_c = _rope_tables(n, C_HEAD_DIM)

    xs = x.reshape(bsz * n, d)
    cs = ctx.reshape(bsz * n_ctx, d)
    q_w = d

    for i in range(depth):
        ctx_out = i < depth - 1
        kind = i % N_MIXERS
        j = i // N_MIXERS
        ng = norm_g[i]
        g0 = ng[0:1]

        def lat_row_fn(tm, i=i):
            return lambda t: i * MOD_ROWS + t // (n // tm)

        lat_row = lat_row_fn(tm_lat)

        def ctx_row_fn(t, i=i):
            return i * MOD_ROWS + ctx_row

        if kind == 0:
            w = a_w_qkv[j]
            kv_cols = KV_HEADS * A_HEAD_DIM
            w_all = jnp.concatenate(
                [w[:, :q_w], _dup_heads(w[:, q_w:q_w + kv_cols], A_HEAD_DIM),
                 _pad_heads(w[:, q_w + kv_cols:], A_HEAD_DIM)], axis=1).astype(BF16)
            q, k, v, nrm = _qkv_call(xs, mod3, lat_row, g0, w_all, tab_a, None,
                                     kind="a", want_q=True, tm=tm_lat, name=f"qkv_a_lat{i}")
            if ctx_out:
                qx, kx, vx, nrm_x = _qkv_call(cs, mod3, ctx_row_fn, g0, w_all, None, None,
                                              kind="a", want_q=True, tm=tm_ctx,
                                              name=f"qkv_a_ctx{i}")
            else:
                kx, vx, nrm_x = _qkv_call(cs, mod3, ctx_row_fn, g0, w_all[:, q_w:], None, None,
                                          kind="a", want_q=False, tm=tm_ctx, name=f"kv_a_ctx{i}")
            sink = a_sink[j]
            nrm_all = jnp.maximum(jnp.max(nrm, axis=0), jnp.max(nrm_x, axis=0))
            bound = 1.03 * jnp.sqrt(jnp.maximum(nrm_all[0], nrm_all[1]) * nrm_all[2])
            use_bound = bound - jnp.min(sink) * LOG2E <= A_SHIFT_HEADROOM

            def att_a(qq, band, tag):
                kk, vv = (k, v) if band else (None, None)
                return lax.cond(
                    use_bound,
                    lambda: _att_a_call(sink, qq, kk, vv, kx, vx, bsz=bsz, band=band, shift=bound,
                                        name=f"att_a_{tag}_bounded{i}"),
                    lambda: _att_a_call(sink, qq, kk, vv, kx, vx, bsz=bsz, band=band,
                                        name=f"att_a_{tag}{i}"))

            a_lat = att_a(q, True, "lat")
            if ctx_out:
                a_ctx = att_a(qx, False, "ctx")
            w_o = a_w_o[j]
        elif kind == 1:
            w_in = b_w_in[j].astype(BF16)
            b_in = b_b_in[j][None, :]
            ln_g = b_ln_g[j][None, :]
            ln_b = b_ln_b[j][None, :]
            w_s = b_w_s[j].astype(BF16)
            bs_tile = jnp.broadcast_to(b_b_s[j][:, :, None], b_b_s[j].shape + (LANES,))
            a_lat = _gmlp_call(xs, mod3, lat_row, g0, w_in, b_in, ln_g, ln_b, w_s, bs_tile,
                               tm=tm_lat, name=f"gmlp_lat{i}")
            if ctx_out:
                a_ctx = _gmlp_call(cs, mod3, ctx_row_fn, g0, w_in, b_in, ln_g, ln_b, w_s, bs_tile,
                                   tm=tm_ctx, name=f"gmlp_ctx{i}")
            w_o = b_w_o[j]
        else:
            w_all = c_w_qkv[j].astype(BF16)
            gains = (c_q_g[j][None, :], c_k_g[j][None, :])
            q, k, v = _qkv_call(xs, mod3, lat_row_fn(tm_qkv_c), g0, w_all, tab_c, gains,
                                kind="c", want_q=True, tm=tm_qkv_c, name=f"qkv_c_lat{i}")
            if ctx_out:
                qx, kx, vx = _qkv_call(cs, mod3, ctx_row_fn, g0, w_all, None, gains,
                                       kind="c", want_q=True, tm=tm_ctx, name=f"qkv_c_ctx{i}")
            else:
                kx, vx = _qkv_call(cs, mod3, ctx_row_fn, g0, w_all[:, q_w:], None, gains,
                                   kind="c", want_q=False, tm=tm_ctx, name=f"kv_c_ctx{i}")
            bound = (C_LOGIT_BOUND * jnp.max(jnp.abs(c_q_g[j])) * jnp.max(jnp.abs(c_k_g[j]))).astype(F32)
            a_lat = lax.cond(
                2.0 * bound < F32_SAFE_EXP2_RANGE,
                lambda: _att_c_call(q, k, v, kx, vx, bsz=bsz, lat=True, tq=tq_c, shift=bound,
                                    name=f"att_c_lat_bounded{i}"),
                lambda: _att_c_call(q, k, v, kx, vx, bsz=bsz, lat=True, tq=256,
                                    name=f"att_c_lat{i}"))
            if ctx_out:
                a_ctx = _att_c_call(qx, None, None, kx, vx, bsz=bsz, lat=False, tq=n_ctx,
                                    name=f"att_c_ctx{i}")
            w_o = c_w_o[j]

        w_o = w_o.astype(BF16)
        f_in = ffn_w_in[i].astype(BF16)
        f_out = ffn_w_out[i].astype(BF16)
        n_chunks = f_out.shape[0] // MXU_TILE
        xs = _post_ffn_call(a_lat, xs, mod3, lat_row_fn(tm_post), ng, w_o, f_in, f_out,
                            tm=tm_post, n_chunks=n_chunks, name=f"post_ffn_lat{i}")
        if ctx_out:
            cs = _post_ffn_call(a_ctx, cs, mod3, ctx_row_fn, ng, w_o, f_in, f_out,
                                tm=tm_post_ctx, n_chunks=n_chunks, name=f"post_ffn_ctx{i}")
    return xs.reshape(bsz, n, d)
```

```python
import functools
import math

import jax
import jax.numpy as jnp
from jax import lax
from jax.experimental import pallas as pl
from jax.experimental.pallas import tpu as pltpu

F32 = jnp.float32
BF16 = jnp.bfloat16

LANES = 128
SUBLANES = 8
MXU_TILE = 256
MIB = 1024 * 1024

GRID_W = 64
N_MIXERS = 3
BLOCK = 128
A_HEAD_DIM = 64
C_HEAD_DIM = 128
KV_HEADS = 4
B_GROUPS = 8
ROPE_THETA = 10000.0
RMS_EPS = 1e-6
LN_EPS = 1e-5
NEG_INF = -1e30
LOG2E = math.log2(math.e)
C_LOGIT_BOUND = 1.02 * C_HEAD_DIM * C_HEAD_DIM ** -0.5 * LOG2E
F32_SAFE_EXP2_RANGE = 120.0
A_SHIFT_HEADROOM = 100.0
A_Q_BLOCKS = 2
POST_SUB_ROWS = 512
MOD_ROWS = 40


def _rms(x, g):
    return x * lax.rsqrt(jnp.mean(x * x, axis=-1, keepdims=True) + RMS_EPS) * g


def _silu(x):
    return x / (1.0 + jnp.exp(-x))


def _const_spec(shape):
    nd = len(shape)
    return pl.BlockSpec(shape, lambda *_: (0,) * nd, pipeline_mode=pl.Buffered(1))


def _params(vmem_mib, n_axes=1):
    return pltpu.CompilerParams(
        dimension_semantics=("parallel",) * n_axes,
        vmem_limit_bytes=vmem_mib * MIB,
    )


def _ada_kernel(c_ref, w_ref, b_ref, o_ref):
    a = _silu(c_ref[...]).astype(BF16)
    o_ref[...] = jnp.dot(a, w_ref[...].astype(BF16), preferred_element_type=F32) + b_ref[...]


def _ada_call(c_rows, ada_w, ada_b):
    depth, d, n6 = ada_w.shape
    tn = 1536
    return pl.pallas_call(
        _ada_kernel,
        grid=(depth, n6 // tn),
        in_specs=[
            pl.BlockSpec((MOD_ROWS, d), lambda i, j: (0, 0)),
            pl.BlockSpec((None, d, tn), lambda i, j: (i, 0, j)),
            pl.BlockSpec((None, 1, tn), lambda i, j: (i, 0, j)),
        ],
        out_specs=pl.BlockSpec((None, MOD_ROWS, tn), lambda i, j: (i, 0, j)),
        out_shape=jax.ShapeDtypeStruct((depth, MOD_ROWS, n6), F32),
        compiler_params=_params(40, 2),
        name="ada_mod",
    )(c_rows, ada_w, ada_b.reshape(depth, 1, n6))


def _qkv_kernel(*refs, kind, rope, want_q, d):
    it = iter(refs)
    x_ref, mod_ref, g_ref, w_ref = next(it), next(it), next(it), next(it)
    cos_ref = sin_ref = qg_ref = kg_ref = q_ref = None
    if rope:
        cos_ref, sin_ref = next(it), next(it)
    if kind == "c":
        qg_ref, kg_ref = next(it), next(it)
    else:
        sel_ref = next(it)
    if want_q:
        q_ref = next(it)
    k_ref, v_ref = next(it), next(it)
    if kind == "a":
        nrm_ref = next(it)

    shift = mod_ref[:, 0:d]
    scale = mod_ref[:, d:2 * d]
    h = _rms(x_ref[...], g_ref[...]) * (1.0 + scale) + shift
    y = jnp.dot(h.astype(BF16), w_ref[...], preferred_element_type=F32)

    q_w = d if want_q else 0
    kv_w = KV_HEADS * LANES
    if rope:
        cos = cos_ref[...]
        sin = sin_ref[...]
        lane = lax.broadcasted_iota(jnp.int32, cos.shape, 1)
        first_half = (lane & (A_HEAD_DIM - 1)) < (A_HEAD_DIM // 2)

    def rotate(blk):
        if kind == "a":
            fwd = pltpu.roll(blk, LANES - A_HEAD_DIM // 2, axis=1)
            bwd = pltpu.roll(blk, A_HEAD_DIM // 2, axis=1)
            return jnp.where(first_half, fwd, bwd)
        return pltpu.roll(blk, C_HEAD_DIM // 2, axis=1)

    def head_block(col, gain_ref, out_scale):
        blk = y[:, col:col + LANES]
        if kind == "c":
            blk = _rms(blk, gain_ref[...])
        if rope:
            blk = blk * cos + rotate(blk) * sin
        if out_scale != 1.0:
            blk = blk * out_scale
        return blk

    def fold_sq(acc, blk):
        sq = blk * blk
        return sq if acc is None else jnp.maximum(acc, sq)

    head_dim = A_HEAD_DIM if kind == "a" else C_HEAD_DIM
    q_scale = head_dim ** -0.5 * LOG2E
    zq = zk = None
    for j in range(q_w // LANES):
        blk = head_block(j * LANES, qg_ref, q_scale)
        q_ref[:, j * LANES:(j + 1) * LANES] = blk.astype(BF16)
        if kind == "a":
            zq = fold_sq(zq, blk)
    for j in range(KV_HEADS):
        blk = head_block(q_w + j * LANES, kg_ref, 1.0)
        k_ref[:, j * LANES:(j + 1) * LANES] = blk.astype(BF16)
        if kind == "a":
            zk = fold_sq(zk, blk)
    if kind == "a":
        if zq is None:
            zq = jnp.zeros_like(zk)
        z = jnp.concatenate([zq, zk], axis=1).astype(BF16)
        sums = jnp.dot(z, sel_ref[...], preferred_element_type=F32)
        nrm_ref[...] = jnp.broadcast_to(jnp.max(sums, axis=0, keepdims=True), nrm_ref.shape)
    vlane = lax.broadcasted_iota(jnp.int32, (y.shape[0], LANES), 1)
    for j in range(KV_HEADS):
        vblk = y[:, q_w + kv_w + j * LANES:q_w + kv_w + (j + 1) * LANES]
        if kind == "a":
            v_ref[:, j * LANES:(j + 1) * LANES] = jnp.where(vlane == A_HEAD_DIM, 1.0, vblk).astype(BF16)
        else:
            v_ref[:, (2 * j) * LANES:(2 * j + 1) * LANES] = vblk.astype(BF16)
            v_ref[:, (2 * j + 1) * LANES:(2 * j + 2) * LANES] = jnp.where(vlane == 0, 1.0, 0.0).astype(BF16)


def _qkv_call(x2, mod3, mod_row, g_row, w, tables, gains, *, kind, want_q, tm, name):
    t_tokens, d = x2.shape
    n_w = w.shape[1]
    kv_w = KV_HEADS * LANES
    rope = tables is not None
    grid = (t_tokens // tm,)
    in_specs = [
        pl.BlockSpec((tm, d), lambda t: (t, 0)),
        pl.BlockSpec((None, 1, mod3.shape[2]), lambda t: (mod_row(t), 0, 0)),
        _const_spec((1, d)),
        _const_spec((d, n_w)),
    ]
    args = [x2, mod3, g_row, w]
    if rope:
        cos, sin = tables
        n_tab = cos.shape[0] // tm
        in_specs += [pl.BlockSpec((tm, LANES), lambda t: (t % n_tab, 0))] * 2
        args += [cos, sin]
    if kind == "c":
        in_specs += [_const_spec((1, LANES))] * 2
        args += list(gains)
    else:
        lane = jnp.arange(2 * LANES)[:, None]
        col = jnp.arange(LANES)[None, :]
        sel = (lane // A_HEAD_DIM == col) & (col < 3)
        in_specs.append(_const_spec((2 * LANES, LANES)))
        args.append(sel.astype(BF16))
    out_specs = []
    out_shape = []
    if want_q:
        out_specs.append(pl.BlockSpec((tm, d), lambda t: (t, 0)))
        out_shape.append(jax.ShapeDtypeStruct((t_tokens, d), BF16))
    for width in (kv_w, kv_w if kind == "a" else 2 * kv_w):
        out_specs.append(pl.BlockSpec((tm, width), lambda t: (t, 0)))
        out_shape.append(jax.ShapeDtypeStruct((t_tokens, width), BF16))
    if kind == "a":
        out_specs.append(pl.BlockSpec((SUBLANES, LANES), lambda t: (t, 0)))
        out_shape.append(jax.ShapeDtypeStruct((grid[0] * SUBLANES, LANES), F32))
    return pl.pallas_call(
        functools.partial(_qkv_kernel, kind=kind, rope=rope, want_q=want_q, d=d),
        grid=grid,
        in_specs=in_specs,
        out_specs=out_specs,
        out_shape=out_shape,
        compiler_params=_params(40),
        name=name,
    )(*args)


def _att_a_kernel(*refs, band, nb, row_max):
    if not row_max:
        shift_ref, refs = refs[0], refs[1:]
    if band:
        sink_ref, q_ref = refs[0], refs[1]
        k_refs = refs[2:2 + A_Q_BLOCKS + 2]
        v_refs = refs[2 + A_Q_BLOCKS + 2:2 + 2 * (A_Q_BLOCKS + 2)]
        kx_ref, vx_ref, o_ref = refs[-3:]
    else:
        sink_ref, q_ref, kx_ref, vx_ref, o_ref = refs
    group = 4
    rows = group * BLOCK
    chunk = BLOCK // 2
    n_ctx = kx_ref.shape[0]
    lane = lax.broadcasted_iota(jnp.int32, (BLOCK, LANES), 1)
    low = lane < A_HEAD_DIM
    zero = jnp.zeros((BLOCK, LANES), BF16)
    if band:
        row = lax.broadcasted_iota(jnp.int32, (chunk, BLOCK), 0)
        col = lax.broadcasted_iota(jnp.int32, (chunk, BLOCK), 1)

    for u, h in [(u, h) for u in range(q_ref.shape[0] // BLOCK) for h in range(KV_HEADS)]:
        qs = slice(u * BLOCK, (u + 1) * BLOCK)
        ks = slice(h * LANES, (h + 1) * LANES)
        if band:
            i = pl.program_id(1) * A_Q_BLOCKS + u
            kb = jnp.concatenate([r[:, ks] for r in k_refs[u:u + 3]] + [kx_ref[:, ks]], axis=0)
            vb = jnp.concatenate([r[:, ks] for r in v_refs[u:u + 3]] + [vx_ref[:, ks]], axis=0)
        else:
            kb = kx_ref[:, ks]
            vb = vx_ref[:, ks]
        b0 = q_ref[qs, (2 * h) * LANES:(2 * h + 1) * LANES]
        b1 = q_ref[qs, (2 * h + 1) * LANES:(2 * h + 2) * LANES]
        lhs = jnp.concatenate(
            [jnp.where(low, b0, zero), jnp.where(low, zero, b0),
             jnp.where(low, b1, zero), jnp.where(low, zero, b1)], axis=0)
        s = lax.dot_general(lhs, kb, (((1,), (1,)), ((), ())), preferred_element_type=F32)
        es = []
        sink_terms = []
        for r0 in range(0, rows, chunk):
            sink = sink_ref[group * h + r0 // BLOCK] * LOG2E
            sc = s[r0:r0 + chunk]
            if band:
                p = row + (r0 % BLOCK)
                parts = [jnp.where((col >= p) & (i > 0), sc[:, 0:BLOCK], NEG_INF),
                         sc[:, BLOCK:2 * BLOCK],
                         jnp.where((col <= p) & (i < nb - 1), sc[:, 2 * BLOCK:3 * BLOCK], NEG_INF)]
                first_ctx = 3 * BLOCK
            else:
                parts = []
                first_ctx = 0
            parts += [sc[:, first_ctx + t * BLOCK:first_ctx + (t + 1) * BLOCK]
                      for t in range(n_ctx // BLOCK)]
            if row_max:
                m = functools.reduce(jnp.maximum, parts)
                m = jnp.maximum(jnp.max(m, axis=1, keepdims=True), sink)
            else:
                m = jnp.full((chunk, 1), jnp.maximum(shift_ref[0], sink), F32)
            es.append(jnp.concatenate([jnp.exp2(part - m).astype(BF16) for part in parts], axis=1))
            sink_terms.append(jnp.exp2(sink - m))
        pv = jnp.dot(jnp.concatenate(es, axis=0), vb, preferred_element_type=F32)
        l = pv[:, A_HEAD_DIM:A_HEAD_DIM + 1] + jnp.concatenate(sink_terms, axis=0)
        pv = pv / l
        o0 = jnp.where(low, pv[0:BLOCK], pltpu.roll(pv[BLOCK:2 * BLOCK], A_HEAD_DIM, axis=1))
        o1 = jnp.where(low, pv[2 * BLOCK:3 * BLOCK], pltpu.roll(pv[3 * BLOCK:4 * BLOCK], A_HEAD_DIM, axis=1))
        o_ref[qs, (2 * h) * LANES:(2 * h + 1) * LANES] = o0.astype(BF16)
        o_ref[qs, (2 * h + 1) * LANES:(2 * h + 2) * LANES] = o1.astype(BF16)


def _att_a_call(sink, q, k, v, kx, vx, *, bsz, band, name, shift=None):
    t_tokens, d = q.shape
    row_max = shift is None
    nb = t_tokens // bsz // BLOCK
    assert nb % A_Q_BLOCKS == 0
    steps = nb // A_Q_BLOCKS
    n_ctx = kx.shape[0] // bsz
    kv_w = KV_HEADS * LANES
    qspec = pl.BlockSpec((A_Q_BLOCKS * BLOCK, d), lambda b, i: (b * steps + i, 0))
    xspec = pl.BlockSpec((n_ctx, kv_w), lambda b, i: (b, 0))
    sspec = pl.BlockSpec(memory_space=pltpu.SMEM)
    if band:
        def neighbour(j):
            return pl.BlockSpec(
                (BLOCK, kv_w),
                lambda b, i: (b * nb + jnp.clip(i * A_Q_BLOCKS - 1 + j, 0, nb - 1), 0))

        bands = [neighbour(j) for j in range(A_Q_BLOCKS + 2)]
        in_specs = [sspec, qspec] + bands + bands + [xspec, xspec]
        args = (sink, q) + (k,) * len(bands) + (v,) * len(bands) + (kx, vx)
    else:
        in_specs = [sspec, qspec, xspec, xspec]
        args = (sink, q, kx, vx)
    if not row_max:
        in_specs = [sspec] + in_specs
        args = (shift.reshape(1).astype(F32),) + args
    return pl.pallas_call(
        functools.partial(_att_a_kernel, band=band, nb=nb, row_max=row_max),
        grid=(bsz, steps),
        in_specs=in_specs,
        out_specs=qspec,
        out_shape=jax.ShapeDtypeStruct((t_tokens, d), BF16),
        compiler_params=_params(32, 2),
        name=name,
    )(*args)


def _att_c_kernel(*refs, lat, row_max):
    if not row_max:
        shift_ref, refs = refs[0], refs[1:]
    if lat:
        q_ref, k_ref, v_ref, kx_ref, vx_ref, o_ref = refs
    else:
        q_ref, kx_ref, vx_ref, o_ref = refs
    tq = q_ref.shape[0]
    nt = (((1,), (1,)), ((), ()))
    for h in range(KV_HEADS):
        ks = slice(h * LANES, (h + 1) * LANES)
        vs = slice(2 * h * LANES, (2 * h + 2) * LANES)
        lhs = jnp.concatenate(
            [q_ref[:, (2 * h) * LANES:(2 * h + 1) * LANES],
             q_ref[:, (2 * h + 1) * LANES:(2 * h + 2) * LANES]], axis=0)
        s_ctx = lax.dot_general(lhs, kx_ref[:, ks], nt, preferred_element_type=F32)
        if lat:
            s_lat = lax.dot_general(lhs, k_ref[:, ks], nt, preferred_element_type=F32)
        if row_max:
            m = jnp.max(s_ctx, axis=1, keepdims=True)
            if lat:
                m = jnp.maximum(m, jnp.max(s_lat, axis=1, keepdims=True))
        else:
            m = shift_ref[0]
        pv = jnp.dot(jnp.exp2(s_ctx - m).astype(BF16), vx_ref[:, vs], preferred_element_type=F32)
        if lat:
            pv = pv + jnp.dot(jnp.exp2(s_lat - m).astype(BF16), v_ref[:, vs],
                              preferred_element_type=F32)
        out = pv[:, :LANES] / pv[:, LANES:LANES + 1]
        o_ref[:, (2 * h) * LANES:(2 * h + 1) * LANES] = out[:tq].astype(BF16)
        o_ref[:, (2 * h + 1) * LANES:(2 * h + 2) * LANES] = out[tq:].astype(BF16)


def _att_c_call(q, k, v, kx, vx, *, bsz, lat, tq, name, shift=None):
    t_tokens, d = q.shape
    row_max = shift is None
    nq = t_tokens // bsz // tq
    n_ctx = kx.shape[0] // bsz
    kv_w = KV_HEADS * LANES
    qspec = pl.BlockSpec((tq, d), lambda b, i: (b * nq + i, 0))
    kxspec = pl.BlockSpec((n_ctx, kv_w), lambda b, i: (b, 0))
    vxspec = pl.BlockSpec((n_ctx, 2 * kv_w), lambda b, i: (b, 0))
    if lat:
        n_lat = k.shape[0] // bsz
        kspec = pl.BlockSpec((n_lat, kv_w), lambda b, i: (b, 0))
        vspec = pl.BlockSpec((n_lat, 2 * kv_w), lambda b, i: (b, 0))
        in_specs = [qspec, kspec, vspec, kxspec, vxspec]
        args = (q, k, v, kx, vx)
    else:
        in_specs = [qspec, kxspec, vxspec]
        args = (q, kx, vx)
    if not row_max:
        in_specs = [pl.BlockSpec(memory_space=pltpu.SMEM)] + in_specs
        args = (shift.reshape(1).astype(F32),) + args
    return pl.pallas_call(
        functools.partial(_att_c_kernel, lat=lat, row_max=row_max),
        grid=(bsz, nq),
        in_specs=in_specs,
        out_specs=qspec,
        out_shape=jax.ShapeDtypeStruct((t_tokens, d), BF16),
        compiler_params=_params(48, 2),
        name=name,
    )(*args)


def _gmlp_kernel(x_ref, mod_ref, g_ref, win_ref, bin_ref, lng_ref, lnb_ref, ws_ref, bs_ref,
                 a_ref, *, d):
    tm = x_ref.shape[0]
    shift = mod_ref[:, 0:d]
    scale = mod_ref[:, d:2 * d]
    h = _rms(x_ref[...], g_ref[...]) * (1.0 + scale) + shift
    z = jnp.dot(h.astype(BF16), win_ref[...], preferred_element_type=F32) + bin_ref[...]
    z = 0.5 * z * (1.0 + lax.erf(z * (2.0 ** -0.5)))
    width = z.shape[1] // 2
    u = z[:, :width]
    v = z[:, width:]
    mu = jnp.mean(v, axis=-1, keepdims=True)
    vc = v - mu
    var = jnp.mean(vc * vc, axis=-1, keepdims=True)
    vn = (vc * lax.rsqrt(var + LN_EPS) * lng_ref[...] + lnb_ref[...]).astype(BF16)
    n_chunk = tm // BLOCK
    for g in range(B_GROUPS):
        gs = slice(g * LANES, (g + 1) * LANES)
        rhs = jnp.concatenate([vn[c * BLOCK:(c + 1) * BLOCK, gs] for c in range(n_chunk)], axis=1)
        mixed = jnp.dot(ws_ref[g], rhs, preferred_element_type=F32)
        bias = bs_ref[g]
        for c in range(n_chunk):
            rs = slice(c * BLOCK, (c + 1) * BLOCK)
            a_ref[rs, gs] = (u[rs, gs] * (mixed[:, c * BLOCK:(c + 1) * BLOCK] + bias)).astype(BF16)


def _gmlp_call(x2, mod3, mod_row, g_row, w_in, b_in, ln_g, ln_b, w_s, b_s_tile, *, tm, name):
    t_tokens, d = x2.shape
    n_in = w_in.shape[1]
    width = n_in // 2
    return pl.pallas_call(
        functools.partial(_gmlp_kernel, d=d),
        grid=(t_tokens // tm,),
        in_specs=[
            pl.BlockSpec((tm, d), lambda t: (t, 0)),
            pl.BlockSpec((None, 1, mod3.shape[2]), lambda t: (mod_row(t), 0, 0)),
            _const_spec((1, d)),
            _const_spec((d, n_in)),
            _const_spec((1, n_in)),
            _const_spec((1, width)),
            _const_spec((1, width)),
            _const_spec(w_s.shape),
            _const_spec(b_s_tile.shape),
        ],
        out_specs=pl.BlockSpec((tm, width), lambda t: (t, 0)),
        out_shape=jax.ShapeDtypeStruct((t_tokens, width), BF16),
        compiler_params=_params(48),
        name=name,
    )(x2, mod3, g_row, w_in, b_in, ln_g, ln_b, w_s, b_s_tile)


def _post_ffn_kernel(a_ref, x_ref, mod_ref, ng_ref, wo_ref, win_ref, wout_ref, o_ref, *,
                     d, n_chunks):
    g_mix = mod_ref[:, 2 * d:3 * d]
    sh_f = mod_ref[:, 3 * d:4 * d]
    sc_f = mod_ref[:, 4 * d:5 * d]
    g_ffn = mod_ref[:, 5 * d:6 * d]
    hidden = wout_ref.shape[0]
    fc = hidden // n_chunks
    tm = x_ref.shape[0]
    sub = min(tm, POST_SUB_ROWS)
    for r0 in range(0, tm, sub):
        rs = slice(r0, r0 + sub)
        y = jnp.dot(a_ref[rs, :], wo_ref[...], preferred_element_type=F32)
        x1 = x_ref[rs, :] + g_mix * _rms(y, ng_ref[1:2, :])
        h = (_rms(x1, ng_ref[2:3, :]) * (1.0 + sc_f) + sh_f).astype(BF16)
        f = None
        for c in range(n_chunks):
            gate = jnp.dot(h, win_ref[:, c * fc:(c + 1) * fc], preferred_element_type=F32)
            up = jnp.dot(h, win_ref[:, hidden + c * fc:hidden + (c + 1) * fc],
                         preferred_element_type=F32)
            act = (_silu(gate) * up).astype(BF16)
            part = jnp.dot(act, wout_ref[c * fc:(c + 1) * fc, :], preferred_element_type=F32)
            f = part if f is None else f + part
        o_ref[rs, :] = x1 + g_ffn * _rms(f, ng_ref[3:4, :])


def _post_ffn_call(a2, x2, mod3, mod_row, ng, w_o, w_in, w_out, *, tm, n_chunks, name):
    t_tokens, d = x2.shape
    return pl.pallas_call(
        functools.partial(_post_ffn_kernel, d=d, n_chunks=n_chunks),
        grid=(t_tokens // tm,),
        in_specs=[
            pl.BlockSpec((tm, a2.shape[1]), lambda t: (t, 0)),
            pl.BlockSpec((tm, d), lambda t: (t, 0)),
            pl.BlockSpec((None, 1, mod3.shape[2]), lambda t: (mod_row(t), 0, 0)),
            _const_spec(ng.shape),
            _const_spec(w_o.shape),
            _const_spec(w_in.shape),
            _const_spec(w_out.shape),
        ],
        out_specs=pl.BlockSpec((tm, d), lambda t: (t, 0)),
        out_shape=jax.ShapeDtypeStruct((t_tokens, d), F32),
        compiler_params=_params(56),
        name=name,
    )(a2, x2, mod3, ng, w_o, w_in, w_out)


def _rope_tables(n_tokens, head_dim):
    rows = n_tokens // GRID_W
    row_pos = jnp.repeat(jnp.arange(rows, dtype=F32), GRID_W)
    col_pos = jnp.tile(jnp.arange(GRID_W, dtype=F32), rows)
    n_freq = head_dim // 4
    inv_freq = ROPE_THETA ** (-jnp.arange(n_freq, dtype=F32) / n_freq)
    angles = jnp.concatenate([row_pos[:, None] * inv_freq, col_pos[:, None] * inv_freq], axis=-1)
    cos = jnp.cos(angles)
    sin = jnp.sin(angles)
    reps = LANES // head_dim
    cos_l = jnp.tile(jnp.concatenate([cos, cos], axis=-1), (1, reps))
    sin_l = jnp.tile(jnp.concatenate([-sin, sin], axis=-1), (1, reps))
    return cos_l, sin_l


def _dup_heads(w, head_dim):
    d, n = w.shape
    reps = LANES // head_dim
    w = w.reshape(d, n // head_dim, 1, head_dim)
    return jnp.broadcast_to(w, (d, n // head_dim, reps, head_dim)).reshape(d, n * reps)


def _pad_heads(w, head_dim):
    d, n = w.shape
    w = w.reshape(d, n // head_dim, head_dim)
    return jnp.pad(w, ((0, 0), (0, 0), (0, LANES - head_dim))).reshape(d, n // head_dim * LANES)


def kernel(x, c, ctx, c_ctx, ada_w, ada_b, norm_g, ffn_w_in, ffn_w_out, a_w_qkv, a_w_o, a_sink,
           b_w_in, b_b_in, b_ln_g, b_ln_b, b_w_s, b_b_s, b_w_o, c_w_qkv, c_w_o, c_q_g, c_k_g):
    bsz, n, d = x.shape
    n_ctx = ctx.shape[1]
    depth = ada_w.shape[0]
    assert bsz < MOD_ROWS and n % 512 == 0 and n_ctx % 256 == 0 and d % LANES == 0

    tm_lat = 512
    tm_qkv_c = 256
    tm_post = 1024 if n % 1024 == 0 else 512
    tq_c = 512
    tm_ctx = 512 if (bsz * n_ctx) % 512 == 0 else 256
    tm_post_ctx = 1024 if (bsz * n_ctx) % 1024 == 0 else tm_ctx
    ctx_row = bsz

    c_rows = jnp.zeros((MOD_ROWS, d), F32).at[:bsz].set(c).at[ctx_row].set(c_ctx)
    mod = _ada_call(c_rows, ada_w, ada_b)
    mod3 = mod.reshape(depth * MOD_ROWS, 1, 6 * d)

    tab_a = _rope_tables(n, A_HEAD_DIM)
    tab_c = _rope_tables(n, C_HEAD_DIM)

    xs = x.reshape(bsz * n, d)
    cs = ctx.reshape(bsz * n_ctx, d)
    q_w = d

    for i in range(depth):
        ctx_out = i < depth - 1
        kind = i % N_MIXERS
        j = i // N_MIXERS
        ng = norm_g[i]
        g0 = ng[0:1]

        def lat_row_fn(tm, i=i):
            return lambda t: i * MOD_ROWS + t // (n // tm)

        lat_row = lat_row_fn(tm_lat)

        def ctx_row_fn(t, i=i):
            return i * MOD_ROWS + ctx_row

        if kind == 0:
            w = a_w_qkv[j]
            kv_cols = KV_HEADS * A_HEAD_DIM
            w_all = jnp.concatenate(
                [w[:, :q_w], _dup_heads(w[:, q_w:q_w + kv_cols], A_HEAD_DIM),
                 _pad_heads(w[:, q_w + kv_cols:], A_HEAD_DIM)], axis=1).astype(BF16)
            q, k, v, nrm = _qkv_call(xs, mod3, lat_row, g0, w_all, tab_a, None,
                                     kind="a", want_q=True, tm=tm_lat, name=f"qkv_a_lat{i}")
            if ctx_out:
                qx, kx, vx, nrm_x = _qkv_call(cs, mod3, ctx_row_fn, g0, w_all, None, None,
                                              kind="a", want_q=True, tm=tm_ctx,
                                              name=f"qkv_a_ctx{i}")
            else:
                kx, vx, nrm_x = _qkv_call(cs, mod3, ctx_row_fn, g0, w_all[:, q_w:], None, None,
                                          kind="a", want_q=False, tm=tm_ctx, name=f"kv_a_ctx{i}")
            sink = a_sink[j]
            nrm_all = jnp.maximum(jnp.max(nrm, axis=0), jnp.max(nrm_x, axis=0))
            bound = 1.03 * jnp.sqrt(jnp.maximum(nrm_all[0], nrm_all[1]) * nrm_all[2])
            use_bound = bound - jnp.min(sink) * LOG2E <= A_SHIFT_HEADROOM

            def att_a(qq, band, tag):
                kk, vv = (k, v) if band else (None, None)
                return lax.cond(
                    use_bound,
                    lambda: _att_a_call(sink, qq, kk, vv, kx, vx, bsz=bsz, band=band, shift=bound,
                                        name=f"att_a_{tag}_bounded{i}"),
                    lambda: _att_a_call(sink, qq, kk, vv, kx, vx, bsz=bsz, band=band,
                                        name=f"att_a_{tag}{i}"))

            a_lat = att_a(q, True, "lat")
            if ctx_out:
                a_ctx = att_a(qx, False, "ctx")
            w_o = a_w_o[j]
        elif kind == 1:
            w_in = b_w_in[j].astype(BF16)
            b_in = b_b_in[j][None, :]
            ln_g = b_ln_g[j][None, :]
            ln_b = b_ln_b[j][None, :]
            w_s = b_w_s[j].astype(BF16)
            bs_tile = jnp.broadcast_to(b_b_s[j][:, :, None], b_b_s[j].shape + (LANES,))
            a_lat = _gmlp_call(xs, mod3, lat_row, g0, w_in, b_in, ln_g, ln_b, w_s, bs_tile,
                               tm=tm_lat, name=f"gmlp_lat{i}")
            if ctx_out:
                a_ctx = _gmlp_call(cs, mod3, ctx_row_fn, g0, w_in, b_in, ln_g, ln_b, w_s, bs_tile,
                                   tm=tm_ctx, name=f"gmlp_ctx{i}")
            w_o = b_w_o[j]
        else:
            w_all = c_w_qkv[j].astype(BF16)
            gains = (c_q_g[j][None, :], c_k_g[j][None, :])
            q, k, v = _qkv_call(xs, mod3, lat_row_fn(tm_qkv_c), g0, w_all, tab_c, gains,
                                kind="c", want_q=True, tm=tm_qkv_c, name=f"qkv_c_lat{i}")
            if ctx_out:
                qx, kx, vx = _qkv_call(cs, mod3, ctx_row_fn, g0, w_all, None, gains,
                                       kind="c", want_q=True, tm=tm_ctx, name=f"qkv_c_ctx{i}")
            else:
                kx, vx = _qkv_call(cs, mod3, ctx_row_fn, g0, w_all[:, q_w:], None, gains,
                                   kind="c", want_q=False, tm=tm_ctx, name=f"kv_c_ctx{i}")
            bound = (C_LOGIT_BOUND * jnp.max(jnp.abs(c_q_g[j])) * jnp.max(jnp.abs(c_k_g[j]))).astype(F32)
            a_lat = lax.cond(
                2.0 * bound < F32_SAFE_EXP2_RANGE,
                lambda: _att_c_call(q, k, v, kx, vx, bsz=bsz, lat=True, tq=tq_c, shift=bound,
                                    name=f"att_c_lat_bounded{i}"),
                lambda: _att_c_call(q, k, v, kx, vx, bsz=bsz, lat=True, tq=256,
                                    name=f"att_c_lat{i}"))
            if ctx_out:
                a_ctx = _att_c_call(qx, None, None, kx, vx, bsz=bsz, lat=False, tq=n_ctx,
                                    name=f"att_c_ctx{i}")
            w_o = c_w_o[j]

        w_o = w_o.astype(BF16)
        f_in = ffn_w_in[i].astype(BF16)
        f_out = ffn_w_out[i].astype(BF16)
        n_chunks = f_out.shape[0] // MXU_TILE
        xs = _post_ffn_call(a_lat, xs, mod3, lat_row_fn(tm_post), ng, w_o, f_in, f_out,
                            tm=tm_post, n_chunks=n_chunks, name=f"post_ffn_lat{i}")
        if ctx_out:
            cs = _post_ffn_call(a_ctx, cs, mod3, ctx_row_fn, ng, w_o, f_in, f_out,
                                tm=tm_post_ctx, n_chunks=n_chunks, name=f"post_ffn_ctx{i}")
    return xs.reshape(bsz, n, d)
```
